```python
import math
import jax, jax.numpy as jnp
from jax import lax
import numpy as np

D_MODEL = 1024
BATCH = 8
SEQ = 2048
DEPTH = 4
DEC_BATCH = 128
DEC_SEQ = 1
PAST_LEN = 2048
PAGE_SIZE = 128

HEAD_DIM = 64
H_DIFF = 8
KV_DIFF = 4
DK_DIFF = HEAD_DIM // 2
DV_DIFF = HEAD_DIM
H_SB = 8
D_SB = HEAD_DIM
ROT_DIM = DK_DIFF // 4
ROPE_THETA = 500000.0
FF_DIM = -(-8 * D_MODEL // (3 * 256)) * 256
Q_DIFF = H_DIFF * 2 * DK_DIFF
K_DIFF = KV_DIFF * 2 * DK_DIFF
V_DIFF = KV_DIFF * DV_DIFF
QKV_SB = H_SB * D_SB
IN_DIM = Q_DIFF + K_DIFF + V_DIFF + 3 * QKV_SB
MIX_DIM = H_DIFF * DV_DIFF + H_SB * D_SB
Q_BLOCK = 128
EPS = 1e-6
NEG = -1e30

kernel_name = 'hymba_diff_stickbreak_decoder_step'


def _rmsnorm(x, g):
    x32 = x.astype(jnp.float32)
    y = x32 * lax.rsqrt(jnp.mean(x32 * x32, axis=-1, keepdims=True) + EPS)
    return (y * g.astype(jnp.float32)).astype(x.dtype)


def _rope(x, pos):
    half = ROT_DIM // 2
    inv = ROPE_THETA ** (-jnp.arange(0, ROT_DIM, 2, dtype=jnp.float32) / ROT_DIM)
    ang = pos.astype(jnp.float32)[:, None] * inv[None, :]
    cos = jnp.cos(ang)[None, :, None, None, :]
    sin = jnp.sin(ang)[None, :, None, None, :]
    xr = x[..., :ROT_DIM].astype(jnp.float32)
    x1, x2 = xr[..., :half], xr[..., half:]
    rot = jnp.concatenate([x1 * cos - x2 * sin, x2 * cos + x1 * sin], axis=-1).astype(x.dtype)
    return jnp.concatenate([rot, x[..., ROT_DIM:]], axis=-1)


def _blocked(fn, q, qpos):
    B, T = q.shape[0], q.shape[1]
    qb = min(Q_BLOCK, T)
    nb = -(-T // qb)
    pad = nb * qb - T
    if pad:
        q = jnp.pad(q, [(0, 0), (0, pad)] + [(0, 0)] * (q.ndim - 2))
        qpos = jnp.pad(qpos, (0, pad), mode='edge')
    qblk = jnp.moveaxis(q.reshape(B, nb, qb, *q.shape[2:]), 1, 0)
    out = lax.map(lambda a: fn(a[0], a[1]), (qblk, qpos.reshape(nb, qb)))
    out = jnp.moveaxis(out, 0, 1)
    return out.reshape(B, nb * qb, *out.shape[3:])[:, :T]


def _diff_attention(q, k, v, qpos, kpos, lam, lam_init, subln_g):
    B, T = q.shape[0], q.shape[1]
    G = H_DIFF // KV_DIFF
    qg = q.reshape(B, T, KV_DIFF, G, 2, DK_DIFF)
    k2 = k.reshape(k.shape[0], k.shape[1], KV_DIFF, 2, DK_DIFF)
    scale = DK_DIFF ** -0.5

    def block(qb, pb):
        s = jnp.einsum('bqhgmd,bkhmd->mbhgqk', qb, k2).astype(jnp.float32) * scale
        mask = kpos[None, :] <= pb[:, None]
        p = jax.nn.softmax(jnp.where(mask, s, NEG), axis=-1)
        w = p[0] - lam * p[1]
        return jnp.einsum('bhgqk,bkhd->bqhgd', w.astype(v.dtype), v)

    o = _blocked(block, qg, qpos)
    o = _rmsnorm(o, subln_g) * (1.0 - lam_init)
    return o.reshape(B, T, H_DIFF * DV_DIFF)


def _sb_attention(q, k, v, qpos, kpos):
    B, T = q.shape[0], q.shape[1]
    scale = D_SB ** -0.5

    def block(qb, pb):
        z = jnp.einsum('bqhd,bkhd->bhqk', qb, k).astype(jnp.float32) * scale
        mask = kpos[None, :] < pb[:, None]
        log_1mb = jnp.where(mask, -jax.nn.softplus(z), 0.0)
        suffix = lax.cumsum(log_1mb, axis=3, reverse=True) - log_1mb
        a = jnp.where(mask, jnp.exp(jax.nn.log_sigmoid(z) + suffix), 0.0)
        return jnp.einsum('bhqk,bkhd->bqhd', a.astype(v.dtype), v)

    o = _blocked(block, q, qpos)
    return o.reshape(B, T, H_SB * D_SB)


def _layer(x, mod, pos, past, p):
    B, T = x.shape[0], x.shape[1]
    sh_a, sc_a, g_a, sh_f, sc_f, g_f = [m[:, None, :] for m in jnp.split(mod, 6, axis=-1)]
    h = _rmsnorm(x, p['norm1_g']) * (1 + sc_a) + sh_a
    proj = h @ p['w_in']
    cuts = [Q_DIFF, Q_DIFF + K_DIFF, Q_DIFF + K_DIFF + V_DIFF,
            Q_DIFF + K_DIFF + V_DIFF + QKV_SB, Q_DIFF + K_DIFF + V_DIFF + 2 * QKV_SB]
    qd, kd, vd, qs, ks, vs = jnp.split(proj, cuts, axis=-1)
    qd = _rope(qd.reshape(B, T, H_DIFF, 2, DK_DIFF), pos)
    kd = _rope(kd.reshape(B, T, KV_DIFF, 2, DK_DIFF), pos).reshape(B, T, KV_DIFF, 2 * DK_DIFF)
    vd = vd.reshape(B, T, KV_DIFF, DV_DIFF)
    qs = qs.reshape(B, T, H_SB, D_SB)
    ks = ks.reshape(B, T, H_SB, D_SB)
    vs = vs.reshape(B, T, H_SB, D_SB)
    new_rows = (kd, vd, ks, vs)
    if past is None:
        kd_all, vd_all, ks_all, vs_all = new_rows
        kpos = pos
    else:
        kd_all, vd_all, ks_all, vs_all = [jnp.concatenate([pp, nn], axis=1) for pp, nn in zip(past, new_rows)]
        kpos = jnp.concatenate([jnp.arange(past[0].shape[1], dtype=pos.dtype), pos])
    o_diff = _diff_attention(qd, kd_all, vd_all, pos, kpos, p['lam'], p['lam_init'], p['subln_g'])
    o_sb = _sb_attention(qs, ks_all, vs_all, pos, kpos)
    o = jnp.concatenate([o_diff, o_sb], axis=-1) @ p['w_o']
    x = x + g_a * o
    h = _rmsnorm(x, p['norm2_g']) * (1 + sc_f) + sh_f
    f = (jax.nn.silu(h @ p['w_gate']) * (h @ p['w_up'])) @ p['w_down']
    x = x + g_f * f
    return x, new_rows


def setup_inputs(seed: int = 0) -> dict:
    key = jax.random.key(seed)
    ks = jax.random.split(key, 24)
    D = D_MODEL
    n_pages = PAST_LEN // PAGE_SIZE
    n_pool = (DEC_BATCH * n_pages * 5) // 4

    def nrm(k, shape, s):
        return jax.random.normal(k, shape, jnp.float32) * s

    x_prompt = nrm(ks[0], (BATCH, SEQ, D), 1.0)
    x_sample = nrm(ks[1], (DEC_BATCH, DEC_SEQ, D), 1.0)
    c_prompt = nrm(ks[2], (BATCH, D), 1.0)
    c_sample = nrm(ks[3], (DEC_BATCH, D), 1.0)
    cache_diff_k = nrm(ks[4], (DEPTH, n_pool, PAGE_SIZE, KV_DIFF, 2 * DK_DIFF), 1.0)
    cache_diff_v = nrm(ks[5], (DEPTH, n_pool, PAGE_SIZE, KV_DIFF, DV_DIFF), 1.0)
    cache_sb_k = nrm(ks[6], (DEPTH, n_pool, PAGE_SIZE, H_SB, D_SB), 1.0)
    cache_sb_v = nrm(ks[7], (DEPTH, n_pool, PAGE_SIZE, H_SB, D_SB), 1.0)
    page_table = jax.random.permutation(ks[8], n_pool)[:DEC_BATCH * n_pages].reshape(DEC_BATCH, n_pages).astype(jnp.int32)
    norm1_g = 1.0 + nrm(ks[9], (DEPTH, D), 0.02)
    norm2_g = 1.0 + nrm(ks[10], (DEPTH, D), 0.02)
    w_ada = nrm(ks[11], (DEPTH, D, 6 * D), 0.1 * D ** -0.5)
    b_ada = nrm(ks[12], (DEPTH, 6 * D), 0.02)
    b_ada = b_ada.at[:, 2 * D:3 * D].add(1.0).at[:, 5 * D:].add(1.0)
    w_in = nrm(ks[13], (DEPTH, D, IN_DIM), D ** -0.5)
    w_o = nrm(ks[14], (DEPTH, MIX_DIM, D), MIX_DIM ** -0.5)
    lam_q1 = nrm(ks[15], (DEPTH, DK_DIFF), 0.1)
    lam_k1 = nrm(ks[16], (DEPTH, DK_DIFF), 0.1)
    lam_q2 = nrm(ks[17], (DEPTH, DK_DIFF), 0.1)
    lam_k2 = nrm(ks[18], (DEPTH, DK_DIFF), 0.1)
    subln_g = 1.0 + nrm(ks[19], (DEPTH, DV_DIFF), 0.02)
    w_gate = nrm(ks[20], (DEPTH, D, FF_DIM), D ** -0.5)
    w_up = nrm(ks[21], (DEPTH, D, FF_DIM), D ** -0.5)
    w_down = nrm(ks[22], (DEPTH, FF_DIM, D), FF_DIM ** -0.5)
    final_g = 1.0 + nrm(ks[23], (D,), 0.02)
    return {'x_prompt': x_prompt, 'x_sample': x_sample, 'c_prompt': c_prompt, 'c_sample': c_sample,
            'cache_diff_k': cache_diff_k, 'cache_diff_v': cache_diff_v,
            'cache_sb_k': cache_sb_k, 'cache_sb_v': cache_sb_v, 'page_table': page_table,
            'norm1_g': norm1_g, 'norm2_g': norm2_g, 'w_ada': w_ada, 'b_ada': b_ada,
            'w_in': w_in, 'w_o': w_o, 'lam_q1': lam_q1, 'lam_k1': lam_k1, 'lam_q2': lam_q2, 'lam_k2': lam_k2,
            'subln_g': subln_g, 'w_gate': w_gate, 'w_up': w_up, 'w_down': w_down, 'final_g': final_g}


def reference(x_prompt, x_sample, c_prompt, c_sample, cache_diff_k, cache_diff_v, cache_sb_k, cache_sb_v,
              page_table, norm1_g, norm2_g, w_ada, b_ada, w_in, w_o, lam_q1, lam_k1, lam_q2, lam_k2,
              subln_g, w_gate, w_up, w_down, final_g):
    dec_b, n_pages = page_table.shape
    past_len = n_pages * cache_diff_k.shape[2]
    pos_p = jnp.arange(x_prompt.shape[1], dtype=jnp.int32)
    pos_s = past_len + jnp.arange(x_sample.shape[1], dtype=jnp.int32)
    sc_p = jax.nn.silu(c_prompt)
    sc_s = jax.nn.silu(c_sample)
    xp, xs = x_prompt, x_sample
    rows_p, rows_s = [], []
    for l in range(DEPTH):
        lam_init = 0.8 - 0.6 * math.exp(-0.3 * l)
        f32 = jnp.float32
        lam = (jnp.exp(jnp.sum(lam_q1[l].astype(f32) * lam_k1[l].astype(f32)))
               - jnp.exp(jnp.sum(lam_q2[l].astype(f32) * lam_k2[l].astype(f32))) + lam_init)
        p = {'norm1_g': norm1_g[l], 'norm2_g': norm2_g[l], 'w_in': w_in[l], 'w_o': w_o[l],
             'lam': lam, 'lam_init': lam_init, 'subln_g': subln_g[l],
             'w_gate': w_gate[l], 'w_up': w_up[l], 'w_down': w_down[l]}
        mod_p = sc_p @ w_ada[l] + b_ada[l]
        mod_s = sc_s @ w_ada[l] + b_ada[l]
        xp, rp = _layer(xp, mod_p, pos_p, None, p)
        past = tuple(c[l][page_table].reshape(dec_b, past_len, c.shape[3], c.shape[4])
                     for c in (cache_diff_k, cache_diff_v, cache_sb_k, cache_sb_v))
        xs, rs = _layer(xs, mod_s, pos_s, past, p)
        rows_p.append(rp)
        rows_s.append(rs)
    y_prompt = _rmsnorm(xp, final_g)
    y_sample = _rmsnorm(xs, final_g)
    new_diff_k_prompt = jnp.stack([r[0] for r in rows_p])
    new_diff_v_prompt = jnp.stack([r[1] for r in rows_p])
    new_sb_k_prompt = jnp.stack([r[2] for r in rows_p])
    new_sb_v_prompt = jnp.stack([r[3] for r in rows_p])
    new_diff_k_sample = jnp.stack([r[0] for r in rows_s])
    new_diff_v_sample = jnp.stack([r[1] for r in rows_s])
    new_sb_k_sample = jnp.stack([r[2] for r in rows_s])
    new_sb_v_sample = jnp.stack([r[3] for r in rows_s])
    return (y_prompt, y_sample, new_diff_k_prompt, new_diff_v_prompt, new_sb_k_prompt, new_sb_v_prompt,
            new_diff_k_sample, new_diff_v_sample, new_sb_k_sample, new_sb_v_sample)
```

```python
import functools
import math

import jax
import jax.numpy as jnp
import numpy as np
from jax import lax
from jax.experimental import pallas as pl
from jax.experimental.pallas import tpu as pltpu

F32 = jnp.float32
BF16 = jnp.bfloat16

HEAD_DIM = 64
H_DIFF = 8
KV_DIFF = 4
DK_DIFF = HEAD_DIM // 2
DV_DIFF = HEAD_DIM
H_SB = 8
D_SB = HEAD_DIM
ROT_DIM = DK_DIFF // 4
ROPE_THETA = 500000.0
EPS = 1e-6
NEG = -1e30
LOG2E = 1.4426950408889634
Q_DIFF = H_DIFF * 2 * DK_DIFF
K_DIFF = KV_DIFF * 2 * DK_DIFF
V_DIFF = KV_DIFF * DV_DIFF
QKV_SB = H_SB * D_SB
KV_ROWS = K_DIFF + V_DIFF + 2 * QKV_SB

LANES = 128
SUBLANES = 8
MXU_DIM = 256
VMEM_LIMIT = 56 * 1024 * 1024

ATT_BLOCK = 256
TOKEN_TILE = 512
FF_CHUNK = 256


def _params(sem, vmem=VMEM_LIMIT):
    return pltpu.CompilerParams(dimension_semantics=sem, vmem_limit_bytes=vmem)


def _dot(a, b):
    return jnp.dot(a, b, preferred_element_type=F32)


def _dot_nt(a, b):
    return lax.dot_general(a, b, (((1,), (1,)), ((), ())), preferred_element_type=F32)


def _silu(x):
    return x * (1.0 / (1.0 + jnp.exp(-x)))


def _norm_mod(x, g, scale, shift):
    ms = jnp.mean(x * x, axis=-1, keepdims=True)
    return (x * lax.rsqrt(ms + EPS)) * g * (1.0 + scale) + shift


def _rmsnorm(x, g):
    ms = jnp.mean(x * x, axis=-1, keepdims=True)
    return (x * lax.rsqrt(ms + EPS)) * g


def _rope_lanes(t, ca, sb, sc):
    return t * ca + pltpu.roll(t, LANES - ROT_DIM // 2, 1) * sb + pltpu.roll(t, ROT_DIM // 2, 1) * sc


def _rope_rows(slab, cos_t, sin_t):
    return slab * cos_t + pltpu.roll(slab, ROT_DIM // 2, 0) * sin_t


def _mod_kernel(c_ref, w_ref, b_ref, o_ref):
    s = _silu(c_ref[...]).astype(BF16)
    o_ref[...] = _dot(s, w_ref[...].astype(BF16)) + b_ref[...]


def _modulation(c_all, w_ada, b_ada):
    depth, d, n = w_ada.shape
    rows = c_all.shape[0]
    tn = 1536
    return pl.pallas_call(
        _mod_kernel,
        grid=(depth, n // tn),
        in_specs=[
            pl.BlockSpec((rows, d), lambda l, j: (0, 0)),
            pl.BlockSpec((None, d, tn), lambda l, j: (l, 0, j)),
            pl.BlockSpec((None, 1, tn), lambda l, j: (l, 0, j)),
        ],
        out_specs=pl.BlockSpec((None, rows, tn), lambda l, j: (l, 0, j)),
        out_shape=jax.ShapeDtypeStruct((depth, rows, n), F32),
        compiler_params=_params(("arbitrary", "arbitrary")),
        name="adaln_mod",
    )(c_all, w_ada, b_ada.reshape(depth, 1, n))


def _lam_kernel(q1_ref, k1_ref, q2_ref, k2_ref, li_ref, o_ref):
    a = jnp.sum(q1_ref[...] * k1_ref[...], axis=-1, keepdims=True)
    b = jnp.sum(q2_ref[...] * k2_ref[...], axis=-1, keepdims=True)
    lam = jnp.exp(a) - jnp.exp(b) + li_ref[...]
    o_ref[...] = jnp.broadcast_to(lam, o_ref.shape)


def _lambdas(lam_q1, lam_k1, lam_q2, lam_k2, lam_init):
    depth = lam_q1.shape[0]
    return pl.pallas_call(
        _lam_kernel,
        out_shape=jax.ShapeDtypeStruct((depth, LANES), F32),
        name="diff_lambda",
    )(lam_q1, lam_k1, lam_q2, lam_k2, lam_init)


def _split_kv_rows(h, wkv_ref, cos_t, sin_t):
    kd = _dot_nt(wkv_ref[0:K_DIFF, :], h)
    parts = []
    for g in range(K_DIFF // DK_DIFF):
        r0 = g * DK_DIFF
        parts.append(_rope_rows(kd[r0:r0 + ROT_DIM, :], cos_t, sin_t))
        parts.append(kd[r0 + ROT_DIM:r0 + DK_DIFF, :])
    kd = jnp.concatenate(parts, axis=0)
    vd = _dot_nt(wkv_ref[K_DIFF:K_DIFF + V_DIFF, :], h)
    ks = _dot_nt(wkv_ref[K_DIFF + V_DIFF:K_DIFF + V_DIFF + QKV_SB, :], h)
    vs = _dot_nt(wkv_ref[K_DIFF + V_DIFF + QKV_SB:KV_ROWS, :], h)
    return kd, vd, ks, vs


def _proj_prompt_kernel(x_ref, mod_ref, g_ref, wq_ref, wkv_ref, ca_ref, sb_ref, sc_ref, ct_ref, st_ref,
                        q_ref, qs_ref, kd_ref, vd_ref, ks_ref, vs_ref,
                        kdb_ref, vdb_ref, ksb_ref, vsb_ref, *, d_model, q_scale, qs_scale):
    d = d_model
    h = _norm_mod(x_ref[...], g_ref[...], mod_ref[:, d:2 * d], mod_ref[:, 0:d]).astype(BF16)
    ca, sb, sc = ca_ref[...], sb_ref[...], sc_ref[...]
    rq = _dot(h, wq_ref[:, 0:Q_DIFF])
    for j in range(Q_DIFF // LANES):
        t = _rope_lanes(rq[:, j * LANES:(j + 1) * LANES], ca, sb, sc)
        q_ref[:, j * LANES:(j + 1) * LANES] = (t * q_scale).astype(BF16)
    qs_ref[...] = (_dot(h, wq_ref[:, Q_DIFF:Q_DIFF + QKV_SB]) * qs_scale).astype(BF16)
    kd, vd, ks, vs = _split_kv_rows(h, wkv_ref, ct_ref[...], st_ref[...])
    kd_ref[...] = kd
    vd_ref[...] = vd
    ks_ref[...] = ks
    vs_ref[...] = vs
    nblk = kdb_ref.shape[0]
    tk = kdb_ref.shape[-1]
    for j in range(nblk):
        sl = slice(j * tk, (j + 1) * tk)
        kdb_ref[j] = kd[:, sl].astype(BF16)
        vdb_ref[j] = vd[:, sl].astype(BF16)
        ksb_ref[j] = ks[:, sl].astype(BF16)
        vsb_ref[j] = vs[:, sl].astype(BF16)


def _proj_prompt(layer, x2, mod_p, g1, wq, wkv, rope_l, rope_t, batch, seq):
    m, d = x2.shape
    tm = min(TOKEN_TILE, seq)
    tk = min(ATT_BLOCK, seq)
    tps = seq // tm
    nblk = tm // tk
    nk = seq // tk
    ca, sb, sc = rope_l
    ct, st = rope_t
    kern = functools.partial(_proj_prompt_kernel, d_model=d,
                             q_scale=DK_DIFF ** -0.5 * LOG2E, qs_scale=D_SB ** -0.5 * LOG2E)
    tok = lambda i: (i // tps, 0, i % tps)
    blk = lambda i: (i // tps, i % tps, 0, 0)
    out_shape = [
        jax.ShapeDtypeStruct((m, Q_DIFF), BF16),
        jax.ShapeDtypeStruct((m, QKV_SB), BF16),
        jax.ShapeDtypeStruct((batch, K_DIFF, seq), F32),
        jax.ShapeDtypeStruct((batch, V_DIFF, seq), F32),
        jax.ShapeDtypeStruct((batch, QKV_SB, seq), F32),
        jax.ShapeDtypeStruct((batch, QKV_SB, seq), F32),
        jax.ShapeDtypeStruct((batch, nk, K_DIFF, tk), BF16),
        jax.ShapeDtypeStruct((batch, nk, V_DIFF, tk), BF16),
        jax.ShapeDtypeStruct((batch, nk, QKV_SB, tk), BF16),
        jax.ShapeDtypeStruct((batch, nk, QKV_SB, tk), BF16),
    ]
    out_specs = [
        pl.BlockSpec((tm, Q_DIFF), lambda i: (i, 0)),
        pl.BlockSpec((tm, QKV_SB), lambda i: (i, 0)),
        pl.BlockSpec((None, K_DIFF, tm), tok),
        pl.BlockSpec((None, V_DIFF, tm), tok),
        pl.BlockSpec((None, QKV_SB, tm), tok),
        pl.BlockSpec((None, QKV_SB, tm), tok),
        pl.BlockSpec((None, nblk, K_DIFF, tk), blk),
        pl.BlockSpec((None, nblk, V_DIFF, tk), blk),
        pl.BlockSpec((None, nblk, QKV_SB, tk), blk),
        pl.BlockSpec((None, nblk, QKV_SB, tk), blk),
    ]
    in_specs = [
        pl.BlockSpec((tm, d), lambda i: (i, 0)),
        pl.BlockSpec((None, None, 1, 6 * d), lambda i: (layer, i // tps, 0, 0)),
        pl.BlockSpec((None, 1, d), lambda i: (layer, 0, 0)),
        pl.BlockSpec((None, d, Q_DIFF + QKV_SB), lambda i: (layer, 0, 0)),
        pl.BlockSpec((None, KV_ROWS, d), lambda i: (layer, 0, 0)),
        pl.BlockSpec((tm, LANES), lambda i: (i % tps, 0)),
        pl.BlockSpec((tm, LANES), lambda i: (i % tps, 0)),
        pl.BlockSpec((tm, LANES), lambda i: (i % tps, 0)),
        pl.BlockSpec((ROT_DIM, tm), lambda i: (0, i % tps)),
        pl.BlockSpec((ROT_DIM, tm), lambda i: (0, i % tps)),
    ]
    return pl.pallas_call(
        kern, grid=(m // tm,), in_specs=in_specs, out_specs=out_specs, out_shape=out_shape,
        compiler_params=_params(("arbitrary",)), name="proj_prompt",
    )(x2, mod_p, g1, wq, wkv, ca, sb, sc, ct, st)


def _diff_attn_kernel(q_ref, k_ref, v_ref, lam_ref, g_ref, o_ref, qm_sc, m_sc, l_sc, acc_sc, *, out_scale):
    tq = q_ref.shape[0]
    tk = k_ref.shape[-1]
    qi = pl.program_id(2)
    lane = lax.broadcasted_iota(jnp.int32, (1, LANES), 1)
    lo = lane < DV_DIFF
    q = q_ref[...]
    zq = jnp.zeros_like(q)
    for gm in range(4):
        qm_sc[gm] = jnp.where((lane >= DK_DIFF * gm) & (lane < DK_DIFF * (gm + 1)), q, zq)
    m_sc[...] = jnp.full(m_sc.shape, NEG, F32)
    l_sc[...] = jnp.zeros(l_sc.shape, F32)
    acc_sc[...] = jnp.zeros(acc_sc.shape, F32)
    zkv = jnp.zeros((DV_DIFF, tk), BF16)
    reps = tk // LANES

    def block(ki, masked):
        kt = k_ref[ki]
        vt = v_ref[ki]
        k2 = jnp.concatenate([kt, kt], axis=0)
        vg = (jnp.concatenate([vt, zkv], axis=0), jnp.concatenate([zkv, vt], axis=0))
        if masked:
            row = lax.broadcasted_iota(jnp.int32, (tq, tk), 0)
            col = lax.broadcasted_iota(jnp.int32, (tq, tk), 1)
            keep = col <= row
        for mp in range(2):
            pv = None
            alphas = []
            for g in range(2):
                gm = 2 * g + mp
                s = _dot(qm_sc[gm], k2)
                if masked:
                    s = jnp.where(keep, s, NEG)
                m_prev = m_sc[gm]
                m_new = jnp.maximum(m_prev, jnp.max(s, axis=1, keepdims=True))
                alpha = jnp.exp2(m_prev - m_new)
                p = jnp.exp2(s - jnp.tile(m_new, (1, reps)))
                l_sc[gm] = alpha * l_sc[gm] + jnp.sum(p, axis=1, keepdims=True)
                m_sc[gm] = m_new
                d = _dot_nt(p.astype(BF16), vg[g])
                pv = d if pv is None else pv + d
                alphas.append(alpha)
            acc_sc[mp] = acc_sc[mp] * jnp.where(lo, alphas[0], alphas[1]) + pv

    def body(ki, carry):
        block(ki, False)
        return carry

    lax.fori_loop(0, qi, body, 0)
    block(qi, True)

    o0 = acc_sc[0] / jnp.where(lo, l_sc[0], l_sc[2])
    o1 = acc_sc[1] / jnp.where(lo, l_sc[1], l_sc[3])
    o = o0 - lam_ref[...] * o1
    ss = o * o
    s_lo = jnp.sum(jnp.where(lo, ss, 0.0), axis=1, keepdims=True)
    s_hi = jnp.sum(jnp.where(lo, 0.0, ss), axis=1, keepdims=True)
    ms = jnp.where(lo, s_lo, s_hi) * (1.0 / DV_DIFF)
    y = (o * lax.rsqrt(ms + EPS)) * g_ref[...] * out_scale
    o_ref[...] = y.astype(o_ref.dtype)


def _diff_attn(layer, q_bf, kdb, vdb, lam, subg2, batch, seq, out_scale):
    tq = kdb.shape[-1]
    nk = kdb.shape[1]
    nq = seq // tq
    kern = functools.partial(_diff_attn_kernel, out_scale=out_scale)
    return pl.pallas_call(
        kern,
        grid=(batch, KV_DIFF, nq),
        in_specs=[
            pl.BlockSpec((tq, LANES), lambda b, h, i: (b * nq + i, h)),
            pl.BlockSpec((None, nk, 2 * DK_DIFF, tq), lambda b, h, i: (b, 0, h, 0)),
            pl.BlockSpec((None, nk, DV_DIFF, tq), lambda b, h, i: (b, 0, h, 0)),
            pl.BlockSpec((None, 1, LANES), lambda b, h, i: (layer, 0, 0)),
            pl.BlockSpec((None, 1, LANES), lambda b, h, i: (layer, 0, 0)),
        ],
        out_specs=pl.BlockSpec((tq, LANES), lambda b, h, i: (b * nq + i, h)),
        out_shape=jax.ShapeDtypeStruct((batch * seq, H_DIFF * DV_DIFF), BF16),
        scratch_shapes=[
            pltpu.VMEM((4, tq, LANES), BF16),
            pltpu.VMEM((4, tq, LANES), F32),
            pltpu.VMEM((4, tq, LANES), F32),
            pltpu.VMEM((2, tq, LANES), F32),
        ],
        compiler_params=_params(("arbitrary", "arbitrary", "arbitrary")),
        name="diff_attn_prompt",
    )(q_bf, kdb, vdb, lam, subg2)


def _sb_attn_kernel(q_ref, k_ref, v_ref, u_ref, o_ref, qm_sc, carry_sc, acc_sc):
    tq = q_ref.shape[0]
    tk = k_ref.shape[-1]
    qi = pl.program_id(2)
    lane = lax.broadcasted_iota(jnp.int32, (1, LANES), 1)
    lo = lane < D_SB
    q = q_ref[...]
    zq = jnp.zeros_like(q)
    qm_sc[0] = jnp.where(lo, q, zq)
    qm_sc[1] = jnp.where(lo, zq, q)
    carry_sc[...] = jnp.zeros(carry_sc.shape, F32)
    acc_sc[...] = jnp.zeros(acc_sc.shape, F32)
    zkv = jnp.zeros((D_SB, tk), BF16)
    reps = tk // LANES

    def block(ki, masked):
        kt = k_ref[ki]
        vt = v_ref[ki]
        vh = (jnp.concatenate([vt[0:D_SB, :], zkv], axis=0), jnp.concatenate([zkv, vt[D_SB:, :]], axis=0))
        u = u_ref[...]
        if masked:
            row = lax.broadcasted_iota(jnp.int32, (tq, tk), 0)
            col = lax.broadcasted_iota(jnp.int32, (tq, tk), 1)
            keep = col < row
        pv = None
        for hh in range(2):
            z = _dot(qm_sc[hh], kt)
            l1 = -(jnp.maximum(z, 0.0) + jnp.log2(1.0 + jnp.exp2(-jnp.abs(z))))
            if masked:
                l1 = jnp.where(keep, l1, 0.0)
            l_hi = l1.astype(BF16)
            l_lo = (l1 - l_hi.astype(F32)).astype(BF16)
            suf = _dot(l_hi, u) + _dot(l_lo, u)
            c = carry_sc[hh]
            a = jnp.exp2(z + suf + jnp.tile(c, (1, reps)))
            if masked:
                a = jnp.where(keep, a, 0.0)
            carry_sc[hh] = c + jnp.broadcast_to(suf[:, 0:1], (tq, LANES))
            d = _dot_nt(a.astype(BF16), vh[hh])
            pv = d if pv is None else pv + d
        acc_sc[...] += pv

    block(qi, True)

    def body(j, carry):
        block(qi - 1 - j, False)
        return carry

    lax.fori_loop(0, qi, body, 0)
    o_ref[...] = acc_sc[...].astype(o_ref.dtype)


def _sb_attn(qs_bf, ksb, vsb, tri, batch, seq):
    tq = ksb.shape[-1]
    nk = ksb.shape[1]
    nq = seq // tq
    return pl.pallas_call(
        _sb_attn_kernel,
        grid=(batch, H_SB // 2, nq),
        in_specs=[
            pl.BlockSpec((tq, LANES), lambda b, h, i: (b * nq + i, h)),
            pl.BlockSpec((None, nk, 2 * D_SB, tq), lambda b, h, i: (b, 0, h, 0)),
            pl.BlockSpec((None, nk, 2 * D_SB, tq), lambda b, h, i: (b, 0, h, 0)),
            pl.BlockSpec((tq, tq), lambda b, h, i: (0, 0)),
        ],
        out_specs=pl.BlockSpec((tq, LANES), lambda b, h, i: (b * nq + i, h)),
        out_shape=jax.ShapeDtypeStruct((batch * seq, H_SB * D_SB), BF16),
        scratch_shapes=[
            pltpu.VMEM((2, tq, LANES), BF16),
            pltpu.VMEM((2, tq, LANES), F32),
            pltpu.VMEM((tq, LANES), F32),
        ],
        compiler_params=_params(("arbitrary", "arbitrary", "arbitrary")),
        name="sb_attn_prompt",
    )(qs_bf, ksb, vsb, tri)


def _ffn_prompt_kernel(x_ref, od_ref, os_ref, mod_ref, g2_ref, wo_ref, wg_ref, wu_ref, wd_ref, fg_ref,
                       o_ref, *, d_model, final):
    d = d_model
    att = _dot(od_ref[...], wo_ref[0:Q_DIFF, :]) + _dot(os_ref[...], wo_ref[Q_DIFF:, :])
    x1 = x_ref[...] + mod_ref[:, 2 * d:3 * d] * att
    h = _norm_mod(x1, g2_ref[...], mod_ref[:, 4 * d:5 * d], mod_ref[:, 3 * d:4 * d]).astype(BF16)
    ff = wg_ref.shape[1]
    acc = None
    for c in range(ff // FF_CHUNK):
        sl = slice(c * FF_CHUNK, (c + 1) * FF_CHUNK)
        a = (_silu(_dot(h, wg_ref[:, sl])) * _dot(h, wu_ref[:, sl])).astype(BF16)
        part = _dot(a, wd_ref[sl, :])
        acc = part if acc is None else acc + part
    y = x1 + mod_ref[:, 5 * d:6 * d] * acc
    if final:
        y = _rmsnorm(y, fg_ref[...])
    o_ref[...] = y


def _ffn_prompt(layer, x2, od, osb, mod_p, g2, wo, wg, wu, wd, fg, seq, final):
    m, d = x2.shape
    tm = min(TOKEN_TILE, seq)
    tps = seq // tm
    ff = wg.shape[-1]
    once = pl.Buffered(1)
    kern = functools.partial(_ffn_prompt_kernel, d_model=d, final=final)
    return pl.pallas_call(
        kern,
        grid=(m // tm,),
        in_specs=[
            pl.BlockSpec((tm, d), lambda i: (i, 0)),
            pl.BlockSpec((tm, Q_DIFF), lambda i: (i, 0)),
            pl.BlockSpec((tm, QKV_SB), lambda i: (i, 0)),
            pl.BlockSpec((None, None, 1, 6 * d), lambda i: (layer, i // tps, 0, 0)),
            pl.BlockSpec((None, 1, d), lambda i: (layer, 0, 0)),
            pl.BlockSpec((None, d, d), lambda i: (layer, 0, 0), pipeline_mode=once),
            pl.BlockSpec((None, d, ff), lambda i: (layer, 0, 0), pipeline_mode=once),
            pl.BlockSpec((None, d, ff), lambda i: (layer, 0, 0), pipeline_mode=once),
            pl.BlockSpec((None, ff, d), lambda i: (layer, 0, 0), pipeline_mode=once),
            pl.BlockSpec((1, d), lambda i: (0, 0)),
        ],
        out_specs=pl.BlockSpec((tm, d), lambda i: (i, 0)),
        out_shape=jax.ShapeDtypeStruct((m, d), F32),
        compiler_params=_params(("arbitrary",)),
        name="ffn_prompt",
    )(x2, od, osb, mod_p, g2, wo, wg, wu, wd, fg)


def _proj_sample_kernel(x_ref, mod_ref, g_ref, wq_ref, wn_ref, wkv_ref, ca_ref, sb_ref, sc_ref, ct_ref, st_ref,
                        qd_ref, qs_ref, kn_ref, vn_ref, kd_ref, vd_ref, ks_ref, vs_ref,
                        *, d_model, q_scale, qs_scale):
    d = d_model
    h = _norm_mod(x_ref[...], g_ref[...], mod_ref[:, d:2 * d], mod_ref[:, 0:d]).astype(BF16)
    ca, sb, sc = ca_ref[...], sb_ref[...], sc_ref[...]
    rq = _dot(h, wq_ref[:, 0:Q_DIFF])
    for j in range(Q_DIFF // LANES):
        qd_ref[:, j * LANES:(j + 1) * LANES] = _rope_lanes(rq[:, j * LANES:(j + 1) * LANES], ca, sb, sc) * q_scale
    qs_ref[...] = _dot(h, wq_ref[:, Q_DIFF:Q_DIFF + QKV_SB]) * qs_scale
    rk = _dot(h, wn_ref[:, 0:K_DIFF])
    for j in range(K_DIFF // LANES):
        kn_ref[:, j * LANES:(j + 1) * LANES] = _rope_lanes(rk[:, j * LANES:(j + 1) * LANES], ca, sb, sc)
    vn_ref[...] = _dot(h, wn_ref[:, K_DIFF:K_DIFF + V_DIFF])
    kd, vd, ks, vs = _split_kv_rows(h, wkv_ref, ct_ref[...], st_ref[...])
    kd_ref[...] = kd
    vd_ref[...] = vd
    ks_ref[...] = ks
    vs_ref[...] = vs


def _proj_sample(layer, xs, mod_all, g1, wq_s, wn, wkv, rope_l, rope_t):
    ns, d = xs.shape
    ca, sb, sc = rope_l
    ct, st = rope_t
    kern = functools.partial(_proj_sample_kernel, d_model=d,
                             q_scale=DK_DIFF ** -0.5 * LOG2E, qs_scale=D_SB ** -0.5 * LOG2E)
    full = lambda shape: pl.BlockSpec(shape, lambda i: (0,) * len(shape))
    out_shape = [
        jax.ShapeDtypeStruct((ns, Q_DIFF), F32),
        jax.ShapeDtypeStruct((ns, QKV_SB), F32),
        jax.ShapeDtypeStruct((ns, K_DIFF), F32),
        jax.ShapeDtypeStruct((ns, V_DIFF), F32),
        jax.ShapeDtypeStruct((K_DIFF, ns), F32),
        jax.ShapeDtypeStruct((V_DIFF, ns), F32),
        jax.ShapeDtypeStruct((QKV_SB, ns), F32),
        jax.ShapeDtypeStruct((QKV_SB, ns), F32),
    ]
    return pl.pallas_call(
        kern,
        grid=(1,),
        in_specs=[
            full((ns, d)),
            pl.BlockSpec((None, ns, 6 * d), lambda i: (layer, 0, 0)),
            pl.BlockSpec((None, 1, d), lambda i: (layer, 0, 0)),
            pl.BlockSpec((None, d, Q_DIFF + QKV_SB), lambda i: (layer, 0, 0)),
            pl.BlockSpec((None, d, K_DIFF + V_DIFF), lambda i: (layer, 0, 0)),
            pl.BlockSpec((None, KV_ROWS, d), lambda i: (layer, 0, 0)),
            full((1, LANES)), full((1, LANES)), full((1, LANES)),
            full((ROT_DIM, 1)), full((ROT_DIM, 1)),
        ],
        out_specs=[full(s.shape) for s in out_shape],
        out_shape=out_shape,
        compiler_params=_params(("arbitrary",)),
        name="proj_sample",
    )(xs, mod_all, g1, wq_s, wn, wkv, ca, sb, sc, ct, st)


def _suffix_sum_lanes(x):
    rows, n = x.shape
    lane = lax.broadcasted_iota(jnp.int32, (rows, LANES), 1)
    tiles = []
    for t in range(n // LANES):
        y = x[:, t * LANES:(t + 1) * LANES]
        sh = 1
        while sh < LANES:
            y = y + jnp.where(lane < LANES - sh, pltpu.roll(y, LANES - sh, 1), 0.0)
            sh *= 2
        tiles.append(y)
    carry = jnp.zeros((rows, 1), F32)
    out = [None] * len(tiles)
    for t in range(len(tiles) - 1, -1, -1):
        out[t] = tiles[t] + carry
        carry = carry + tiles[t][:, 0:1]
    return jnp.concatenate(out, axis=1)


def _decode_attn_kernel(pt_ref, qd_ref, qs_ref, kn_ref, vn_ref, lam_ref, ckd, cvd, cks, cvs,
                        od_ref, os_ref, kd_buf, vd_buf, ks_buf, vs_buf, sem, *, layer, n_pages, page):
    i = pl.program_id(0)
    n = pl.num_programs(0)
    slot = lax.rem(i, 2)
    pairs = ((ckd, kd_buf), (cvd, vd_buf), (cks, ks_buf), (cvs, vs_buf))

    def copies(seq, sl):
        out = []
        for j in range(n_pages):
            pg = pt_ref[seq, j]
            for c, (src, dst) in enumerate(pairs):
                out.append(pltpu.make_async_copy(src.at[layer, pg], dst.at[sl, :, pl.ds(j * page, page)],
                                                 sem.at[sl, c]))
        return out

    @pl.when(i == 0)
    def _():
        for cp in copies(0, 0):
            cp.start()

    @pl.when(i + 1 < n)
    def _():
        for cp in copies(i + 1, 1 - slot):
            cp.start()

    for cp in copies(i, slot):
        cp.wait()

    r8 = lax.broadcasted_iota(jnp.int32, (SUBLANES, K_DIFF), 0)
    c8 = lax.broadcasted_iota(jnp.int32, (SUBLANES, K_DIFF), 1)
    sel = (c8 >> 5) == r8
    qrow = qd_ref[pl.ds(i, 1), :]
    qm = jnp.concatenate(
        [jnp.where(sel, jnp.broadcast_to(qrow[:, g * K_DIFF:(g + 1) * K_DIFF], (SUBLANES, K_DIFF)), 0.0)
         for g in range(2)], axis=0)
    s = _dot(qm, kd_buf[slot])
    kn = kn_ref[pl.ds(i, 1), :]
    s_self = jnp.sum(qm * kn, axis=1, keepdims=True)
    mx = jnp.maximum(jnp.max(s, axis=1, keepdims=True), s_self)
    p = jnp.exp2(s - mx)
    p_self = jnp.exp2(s_self - mx)
    den = jnp.sum(p, axis=1, keepdims=True) + p_self
    o = _dot_nt(p, vd_buf[slot]) + p_self * vn_ref[pl.ds(i, 1), :]
    r16 = lax.broadcasted_iota(jnp.int32, (2 * SUBLANES, V_DIFF), 0)
    c16 = lax.broadcasted_iota(jnp.int32, (2 * SUBLANES, V_DIFF), 1)
    lam = lam_ref[:, 0:1]
    coef = jnp.where((r16[:, 0:1] & 1) == 1, -lam, 1.0) / den
    w = jnp.where((c16 >> 6) == ((r16 & 7) >> 1), o * coef, 0.0)
    od_ref[pl.ds(i, 1), :] = jnp.concatenate(
        [jnp.sum(w[g * SUBLANES:(g + 1) * SUBLANES], axis=0, keepdims=True) for g in range(2)], axis=1)

    rs = lax.broadcasted_iota(jnp.int32, (H_SB, QKV_SB), 0)
    cs = lax.broadcasted_iota(jnp.int32, (H_SB, QKV_SB), 1)
    hsel = (cs >> 6) == rs
    qsm = jnp.where(hsel, jnp.broadcast_to(qs_ref[pl.ds(i, 1), :], (H_SB, QKV_SB)), 0.0)
    z = _dot(qsm, ks_buf[slot])
    l1 = -(jnp.maximum(z, 0.0) + jnp.log2(1.0 + jnp.exp2(-jnp.abs(z))))
    a = jnp.exp2(z + _suffix_sum_lanes(l1))
    osb = _dot_nt(a, vs_buf[slot])
    os_ref[pl.ds(i, 1), :] = jnp.sum(jnp.where(hsel, osb, 0.0), axis=0, keepdims=True)


def _decode_attn(layer, page_table, qd, qs, kn, vn, lam, ckd, cvd, cks, cvs):
    ns, n_pages = page_table.shape
    page = ckd.shape[-1]
    past = n_pages * page
    kern = functools.partial(_decode_attn_kernel, layer=layer, n_pages=n_pages, page=page)
    res = lambda shape: pl.BlockSpec(shape, lambda i, pt: (0,) * len(shape))
    any_spec = pl.BlockSpec(memory_space=pl.ANY)
    grid_spec = pltpu.PrefetchScalarGridSpec(
        num_scalar_prefetch=1,
        grid=(ns,),
        in_specs=[
            res((ns, Q_DIFF)), res((ns, QKV_SB)), res((ns, K_DIFF)), res((ns, V_DIFF)),
            pl.BlockSpec((None, 1, LANES), lambda i, pt: (layer, 0, 0)),
            any_spec, any_spec, any_spec, any_spec,
        ],
        out_specs=[res((ns, H_DIFF * DV_DIFF)), res((ns, QKV_SB))],
        scratch_shapes=[
            pltpu.VMEM((2, K_DIFF, past), F32),
            pltpu.VMEM((2, V_DIFF, past), F32),
            pltpu.VMEM((2, QKV_SB, past), F32),
            pltpu.VMEM((2, QKV_SB, past), F32),
            pltpu.SemaphoreType.DMA((2, 4)),
        ],
    )
    return pl.pallas_call(
        kern,
        grid_spec=grid_spec,
        out_shape=[jax.ShapeDtypeStruct((ns, H_DIFF * DV_DIFF), F32), jax.ShapeDtypeStruct((ns, QKV_SB), F32)],
        compiler_params=_params(("arbitrary",)),
        name="decode_attn",
    )(page_table, qd, qs, kn, vn, lam, ckd, cvd, cks, cvs)


def _ffn_sample_kernel(x_ref, od_ref, os_ref, mod_ref, g2_ref, gs_ref, seg_ref, wo_ref, wg_ref, wu_ref, wd_ref,
                       fg_ref, o_ref, x1_sc, h_sc, acc_sc, *, d_model, out_scale, final):
    d = d_model
    c = pl.program_id(0)

    @pl.when(c == 0)
    def _():
        o = od_ref[...]
        ss = o * o
        s_hi = ss.astype(BF16)
        s_lo = (ss - s_hi.astype(F32)).astype(BF16)
        seg = _dot(s_hi, seg_ref[...]) + _dot(s_lo, seg_ref[...])
        y = (o * lax.rsqrt(seg * (1.0 / DV_DIFF) + EPS)) * gs_ref[...] * out_scale
        att = _dot(y.astype(BF16), wo_ref[0:Q_DIFF, :]) + _dot(os_ref[...].astype(BF16), wo_ref[Q_DIFF:, :])
        x1 = x_ref[...] + mod_ref[:, 2 * d:3 * d] * att
        x1_sc[...] = x1
        h_sc[...] = _norm_mod(x1, g2_ref[...], mod_ref[:, 4 * d:5 * d], mod_ref[:, 3 * d:4 * d]).astype(BF16)
        acc_sc[...] = jnp.zeros(acc_sc.shape, F32)

    h = h_sc[...]
    a = (_silu(_dot(h, wg_ref[...])) * _dot(h, wu_ref[...])).astype(BF16)
    acc_sc[...] += _dot(a, wd_ref[...])

    @pl.when(c == pl.num_programs(0) - 1)
    def _():
        y = x1_sc[...] + mod_ref[:, 5 * d:6 * d] * acc_sc[...]
        if final:
            y = _rmsnorm(y, fg_ref[...])
        o_ref[...] = y


def _ffn_sample(layer, xs, od, osb, mod_all, g2, subg8, seg, wo_s, wg, wu, wd, fg, out_scale, final):
    ns, d = xs.shape
    ff = wg.shape[-1]
    kern = functools.partial(_ffn_sample_kernel, d_model=d, out_scale=out_scale, final=final)
    full = lambda shape: pl.BlockSpec(shape, lambda c: (0,) * len(shape))
    return pl.pallas_call(
        kern,
        grid=(ff // FF_CHUNK,),
        in_specs=[
            full((ns, d)), full((ns, Q_DIFF)), full((ns, QKV_SB)),
            pl.BlockSpec((None, ns, 6 * d), lambda c: (layer, 0, 0)),
            pl.BlockSpec((None, 1, d), lambda c: (layer, 0, 0)),
            pl.BlockSpec((None, 1, Q_DIFF), lambda c: (layer, 0, 0)),
            full((Q_DIFF, Q_DIFF)),
            pl.BlockSpec((None, d, d), lambda c: (layer, 0, 0)),
            pl.BlockSpec((None, d, FF_CHUNK), lambda c: (layer, 0, c)),
            pl.BlockSpec((None, d, FF_CHUNK), lambda c: (layer, 0, c)),
            pl.BlockSpec((None, FF_CHUNK, d), lambda c: (layer, c, 0)),
            full((1, d)),
        ],
        out_specs=full((ns, d)),
        out_shape=jax.ShapeDtypeStruct((ns, d), F32),
        scratch_shapes=[pltpu.VMEM((ns, d), F32), pltpu.VMEM((ns, d), BF16), pltpu.VMEM((ns, d), F32)],
        compiler_params=_params(("arbitrary",)),
        name="ffn_sample",
    )(xs, od, osb, mod_all, g2, subg8, seg, wo_s, wg, wu, wd, fg)


def _rope_tables(pos):
    half = ROT_DIM // 2
    inv = ROPE_THETA ** (-jnp.arange(0, ROT_DIM, 2, dtype=F32) / ROT_DIM)
    ang = pos.astype(F32)[:, None] * inv[None, :]
    cos, sin = jnp.cos(ang), jnp.sin(ang)
    dl = np.arange(LANES) % DK_DIFF
    jl = dl % half
    ca = jnp.where(dl < ROT_DIM, cos[:, jl], 1.0)
    sb = jnp.where(dl < half, -sin[:, jl], 0.0)
    sc = jnp.where((dl >= half) & (dl < ROT_DIM), sin[:, jl], 0.0)
    dr = np.arange(ROT_DIM)
    ct = cos[:, dr % half].T
    st = jnp.where((dr < half)[:, None], -sin[:, dr % half].T, sin[:, dr % half].T)
    return (ca, sb, sc), (ct, st)


def kernel(x_prompt, x_sample, c_prompt, c_sample, cache_diff_k, cache_diff_v, cache_sb_k, cache_sb_v,
           page_table, norm1_g, norm2_g, w_ada, b_ada, w_in, w_o, lam_q1, lam_k1, lam_q2, lam_k2,
           subln_g, w_gate, w_up, w_down, final_g):
    batch, seq, d = x_prompt.shape
    ns = x_sample.shape[0]
    depth = w_in.shape[0]
    n_pool, page = cache_diff_k.shape[1], cache_diff_k.shape[2]
    n_pages = page_table.shape[1]
    past = n_pages * page
    assert x_sample.shape[1] == 1 and seq % min(ATT_BLOCK, seq) == 0

    c0, c1, c2, c3, c4 = Q_DIFF, Q_DIFF + K_DIFF, Q_DIFF + K_DIFF + V_DIFF, \
        Q_DIFF + K_DIFF + V_DIFF + QKV_SB, Q_DIFF + K_DIFF + V_DIFF + 2 * QKV_SB
    w_in_b = w_in.astype(BF16)
    wq = jnp.concatenate([w_in_b[:, :, :c0], w_in_b[:, :, c2:c3]], axis=2)
    perm = np.array([128 * h + 64 * g + e for g in range(2) for h in range(KV_DIFF) for e in range(64)])
    wq_s = jnp.concatenate([w_in_b[:, :, :c0][:, :, perm], w_in_b[:, :, c2:c3]], axis=2)
    wn = w_in_b[:, :, c0:c2]
    wkv = jnp.swapaxes(jnp.concatenate([w_in_b[:, :, c0:c2], w_in_b[:, :, c3:]], axis=2), 1, 2)
    wo = w_o.astype(BF16)
    wo_s = jnp.concatenate([wo[:, :Q_DIFF][:, perm], wo[:, Q_DIFF:]], axis=1)
    wg, wu, wd = w_gate.astype(BF16), w_up.astype(BF16), w_down.astype(BF16)

    g1 = norm1_g.reshape(depth, 1, d)
    g2 = norm2_g.reshape(depth, 1, d)
    fg = final_g.reshape(1, d)
    subg2 = jnp.tile(subln_g, (1, 2)).reshape(depth, 1, 2 * DV_DIFF)
    subg8 = jnp.tile(subln_g, (1, H_DIFF)).reshape(depth, 1, H_DIFF * DV_DIFF)
    lam_inits = [0.8 - 0.6 * math.exp(-0.3 * l) for l in range(depth)]
    lam = _lambdas(lam_q1, lam_k1, lam_q2, lam_k2, jnp.asarray(lam_inits, F32).reshape(depth, 1))
    lam = lam.reshape(depth, 1, LANES)

    mod_all = _modulation(jnp.concatenate([c_sample, c_prompt], axis=0), w_ada, b_ada)
    mod_p = mod_all[:, ns:].reshape(depth, batch, 1, 6 * d)

    rope_pl, rope_pt = _rope_tables(jnp.arange(seq, dtype=jnp.int32))
    rope_sl, rope_st = _rope_tables(past + jnp.arange(1, dtype=jnp.int32))

    tk = min(ATT_BLOCK, seq)
    tri = (np.arange(tk)[:, None] >= np.arange(tk)[None, :]).astype(np.float32)
    tri = jnp.asarray(tri, BF16)
    segm = (np.arange(Q_DIFF)[:, None] // DV_DIFF == np.arange(Q_DIFF)[None, :] // DV_DIFF).astype(np.float32)
    segm = jnp.asarray(segm, BF16)

    def cache_rows(c):
        return jnp.transpose(c, (0, 1, 3, 4, 2)).reshape(depth, n_pool, c.shape[3] * c.shape[4], page)

    ckd, cvd, cks, cvs = (cache_rows(c) for c in (cache_diff_k, cache_diff_v, cache_sb_k, cache_sb_v))

    xp = x_prompt.reshape(batch * seq, d)
    xs = x_sample.reshape(ns, d)
    rows_p, rows_s = [], []
    for l in range(depth):
        out_scale = 1.0 - lam_inits[l]
        final = l == depth - 1
        (q_bf, qs_bf, kd, vd, ks, vs, kdb, vdb, ksb, vsb) = _proj_prompt(
            l, xp, mod_p, g1, wq, wkv, rope_pl, rope_pt, batch, seq)
        od = _diff_attn(l, q_bf, kdb, vdb, lam, subg2, batch, seq, out_scale)
        osb = _sb_attn(qs_bf, ksb, vsb, tri, batch, seq)
        xp = _ffn_prompt(l, xp, od, osb, mod_p, g2, wo, wg, wu, wd, fg, seq, final)
        rows_p.append((kd, vd, ks, vs))

        (qd_s, qs_s, kn_s, vn_s, kd_s, vd_s, ks_s, vs_s) = _proj_sample(
            l, xs, mod_all, g1, wq_s, wn, wkv, rope_sl, rope_st)
        od_s, os_s = _decode_attn(l, page_table, qd_s, qs_s, kn_s, vn_s, lam, ckd, cvd, cks, cvs)
        xs = _ffn_sample(l, xs, od_s, os_s, mod_all, g2, subg8, segm, wo_s, wg, wu, wd, fg, out_scale, final)
        rows_s.append((kd_s, vd_s, ks_s, vs_s))

    def prompt_rows(idx, heads):
        a = jnp.stack([r[idx] for r in rows_p])
        return jnp.transpose(a.reshape(depth, batch, heads, HEAD_DIM, seq), (0, 1, 4, 2, 3))

    def sample_rows(idx, heads):
        a = jnp.stack([r[idx] for r in rows_s])
        return jnp.transpose(a.reshape(depth, heads, HEAD_DIM, ns), (0, 3, 1, 2))[:, :, None]

    return (xp.reshape(batch, seq, d), xs.reshape(ns, 1, d),
            prompt_rows(0, KV_DIFF), prompt_rows(1, KV_DIFF), prompt_rows(2, H_SB), prompt_rows(3, H_SB),
            sample_rows(0, KV_DIFF), sample_rows(1, KV_DIFF), sample_rows(2, H_SB), sample_rows(3, H_SB))
```

```python
import functools
import math

import jax
import jax.numpy as jnp
import numpy as np
from jax import lax
from jax.experimental import pallas as pl
from jax.experimental.pallas import tpu as pltpu

F32 = jnp.float32
BF16 = jnp.bfloat16

HEAD_DIM = 64
H_DIFF = 8
KV_DIFF = 4
DK_DIFF = HEAD_DIM // 2
DV_DIFF = HEAD_DIM
H_SB = 8
D_SB = HEAD_DIM
ROT_DIM = DK_DIFF // 4
ROPE_THETA = 500000.0
EPS = 1e-6
NEG = -1e30
LOG2E = 1.4426950408889634
Q_DIFF = H_DIFF * 2 * DK_DIFF
K_DIFF = KV_DIFF * 2 * DK_DIFF
V_DIFF = KV_DIFF * DV_DIFF
QKV_SB = H_SB * D_SB
KV_ROWS = K_DIFF + V_DIFF + 2 * QKV_SB

LANES = 128
SUBLANES = 8
MXU_DIM = 256
VMEM_LIMIT = 56 * 1024 * 1024

ATT_BLOCK = 256
TOKEN_TILE = 512
FF_CHUNK = 256
SB_STOP_LOG2 = 100.0


def _params(sem, vmem=VMEM_LIMIT):
    return pltpu.CompilerParams(dimension_semantics=sem, vmem_limit_bytes=vmem)


def _dot(a, b):
    return jnp.dot(a, b, preferred_element_type=F32)


def _dot_nt(a, b):
    return lax.dot_general(a, b, (((1,), (1,)), ((), ())), preferred_element_type=F32)


def _silu(x):
    return x * (1.0 / (1.0 + jnp.exp(-x)))


def _norm_mod(x, g, scale, shift):
    ms = jnp.mean(x * x, axis=-1, keepdims=True)
    return (x * lax.rsqrt(ms + EPS)) * g * (1.0 + scale) + shift


def _rmsnorm(x, g):
    ms = jnp.mean(x * x, axis=-1, keepdims=True)
    return (x * lax.rsqrt(ms + EPS)) * g


def _log2_one_minus_beta(z):
    return jnp.log(1.0 + jnp.exp2(-jnp.abs(z))) * (-LOG2E) - jnp.maximum(z, 0.0)


def _rope_lanes(t, ca, sb, sc):
    return t * ca + pltpu.roll(t, LANES - ROT_DIM // 2, 1) * sb + pltpu.roll(t, ROT_DIM // 2, 1) * sc


def _rope_rows(slab, cos_t, sin_t):
    return slab * cos_t + pltpu.roll(slab, ROT_DIM // 2, 0) * sin_t


def _mod_kernel(c_ref, w_ref, b_ref, o_ref):
    s = _silu(c_ref[...]).astype(BF16)
    o_ref[...] = _dot(s, w_ref[...].astype(BF16)) + b_ref[...]


def _modulation(c_all, w_ada, b_ada):
    depth, d, n = w_ada.shape
    rows = c_all.shape[0]
    tn = 1536
    return pl.pallas_call(
        _mod_kernel,
        grid=(depth, n // tn),
        in_specs=[
            pl.BlockSpec((rows, d), lambda l, j: (0, 0)),
            pl.BlockSpec((None, d, tn), lambda l, j: (l, 0, j)),
            pl.BlockSpec((None, 1, tn), lambda l, j: (l, 0, j)),
        ],
        out_specs=pl.BlockSpec((None, rows, tn), lambda l, j: (l, 0, j)),
        out_shape=jax.ShapeDtypeStruct((depth, rows, n), F32),
        compiler_params=_params(("arbitrary", "arbitrary")),
        name="adaln_mod",
    )(c_all, w_ada, b_ada.reshape(depth, 1, n))


def _lam_kernel(q1_ref, k1_ref, q2_ref, k2_ref, li_ref, o_ref):
    a = jnp.sum(q1_ref[...] * k1_ref[...], axis=-1, keepdims=True)
    b = jnp.sum(q2_ref[...] * k2_ref[...], axis=-1, keepdims=True)
    lam = jnp.exp(a) - jnp.exp(b) + li_ref[...]
    o_ref[...] = jnp.broadcast_to(lam, o_ref.shape)


def _lambdas(lam_q1, lam_k1, lam_q2, lam_k2, lam_init):
    depth = lam_q1.shape[0]
    return pl.pallas_call(
        _lam_kernel,
        out_shape=jax.ShapeDtypeStruct((depth, LANES), F32),
        name="diff_lambda",
    )(lam_q1, lam_k1, lam_q2, lam_k2, lam_init)


def _rope_all_rows(x, cos_t, sin_t):
    parts = []
    for g in range(x.shape[0] // DK_DIFF):
        r0 = g * DK_DIFF
        parts.append(_rope_rows(x[r0:r0 + ROT_DIM, :], cos_t, sin_t))
        parts.append(x[r0 + ROT_DIM:r0 + DK_DIFF, :])
    return jnp.concatenate(parts, axis=0)


def _split_kv_rows(h, wkv_ref, cos_t, sin_t):
    kd = _rope_all_rows(_dot_nt(wkv_ref[0:K_DIFF, :], h), cos_t, sin_t)
    vd = _dot_nt(wkv_ref[K_DIFF:K_DIFF + V_DIFF, :], h)
    ks = _dot_nt(wkv_ref[K_DIFF + V_DIFF:K_DIFF + V_DIFF + QKV_SB, :], h)
    vs = _dot_nt(wkv_ref[K_DIFF + V_DIFF + QKV_SB:KV_ROWS, :], h)
    return kd, vd, ks, vs


def _proj_prompt_kernel(x_ref, mod_ref, g_ref, wqt_ref, wkv_ref, wkn_ref, ca_ref, sb_ref, sc_ref, ct_ref, st_ref,
                        qt_ref, qst_ref, kn_ref, ksn_ref, kd_ref, vd_ref, ks_ref, vs_ref, vdb_ref, vsb_ref,
                        *, d_model, q_scale, qs_scale):
    d = d_model
    h = _norm_mod(x_ref[...], g_ref[...], mod_ref[:, d:2 * d], mod_ref[:, 0:d]).astype(BF16)
    ct, st = ct_ref[...], st_ref[...]
    qt = _rope_all_rows(_dot_nt(wqt_ref[0:Q_DIFF, :], h), ct, st)
    qt_ref[...] = (qt * q_scale).astype(BF16)
    qst_ref[...] = (_dot_nt(wqt_ref[Q_DIFF:Q_DIFF + QKV_SB, :], h) * qs_scale).astype(BF16)
    ca, sb, sc = ca_ref[...], sb_ref[...], sc_ref[...]
    rk = _dot(h, wkn_ref[:, 0:K_DIFF])
    for j in range(K_DIFF // LANES):
        sl = slice(j * LANES, (j + 1) * LANES)
        kn_ref[:, sl] = _rope_lanes(rk[:, sl], ca, sb, sc).astype(BF16)
    ksn_ref[...] = _dot(h, wkn_ref[:, K_DIFF:K_DIFF + QKV_SB]).astype(BF16)
    kd, vd, ks, vs = _split_kv_rows(h, wkv_ref, ct, st)
    kd_ref[...] = kd
    vd_ref[...] = vd
    ks_ref[...] = ks
    vs_ref[...] = vs
    nblk = vdb_ref.shape[0]
    tk = vdb_ref.shape[-1]
    for j in range(nblk):
        sl = slice(j * tk, (j + 1) * tk)
        vdb_ref[j] = vd[:, sl].astype(BF16)
        vsb_ref[j] = vs[:, sl].astype(BF16)


def _proj_prompt(layer, x2, mod_p, g1, wqt, wkv, wkn, rope_l, rope_t, batch, seq):
    m, d = x2.shape
    tm = min(TOKEN_TILE, seq)
    tk = min(ATT_BLOCK, seq)
    tps = seq // tm
    nblk = tm // tk
    nk = seq // tk
    ca, sb, sc = rope_l
    ct, st = rope_t
    kern = functools.partial(_proj_prompt_kernel, d_model=d,
                             q_scale=DK_DIFF ** -0.5 * LOG2E, qs_scale=D_SB ** -0.5 * LOG2E)
    tok = lambda i: (i // tps, 0, i % tps)
    blk = lambda i: (i // tps, i % tps, 0, 0)
    out_shape = [
        jax.ShapeDtypeStruct((batch, Q_DIFF, seq), BF16),
        jax.ShapeDtypeStruct((batch, QKV_SB, seq), BF16),
        jax.ShapeDtypeStruct((m, K_DIFF), BF16),
        jax.ShapeDtypeStruct((m, QKV_SB), BF16),
        jax.ShapeDtypeStruct((batch, K_DIFF, seq), F32),
        jax.ShapeDtypeStruct((batch, V_DIFF, seq), F32),
        jax.ShapeDtypeStruct((batch, QKV_SB, seq), F32),
        jax.ShapeDtypeStruct((batch, QKV_SB, seq), F32),
        jax.ShapeDtypeStruct((batch, nk, V_DIFF, tk), BF16),
        jax.ShapeDtypeStruct((batch, nk, QKV_SB, tk), BF16),
    ]
    out_specs = [
        pl.BlockSpec((None, Q_DIFF, tm), tok),
        pl.BlockSpec((None, QKV_SB, tm), tok),
        pl.BlockSpec((tm, K_DIFF), lambda i: (i, 0)),
        pl.BlockSpec((tm, QKV_SB), lambda i: (i, 0)),
        pl.BlockSpec((None, K_DIFF, tm), tok),
        pl.BlockSpec((None, V_DIFF, tm), tok),
        pl.BlockSpec((None, QKV_SB, tm), tok),
        pl.BlockSpec((None, QKV_SB, tm), tok),
        pl.BlockSpec((None, nblk, V_DIFF, tk), blk),
        pl.BlockSpec((None, nblk, QKV_SB, tk), blk),
    ]
    in_specs = [
        pl.BlockSpec((tm, d), lambda i: (i, 0)),
        pl.BlockSpec((None, None, 1, 6 * d), lambda i: (layer, i // tps, 0, 0)),
        pl.BlockSpec((None, 1, d), lambda i: (layer, 0, 0)),
        pl.BlockSpec((None, Q_DIFF + QKV_SB, d), lambda i: (layer, 0, 0)),
        pl.BlockSpec((None, KV_ROWS, d), lambda i: (layer, 0, 0)),
        pl.BlockSpec((None, d, K_DIFF + QKV_SB), lambda i: (layer, 0, 0)),
        pl.BlockSpec((tm, LANES), lambda i: (i % tps, 0)),
        pl.BlockSpec((tm, LANES), lambda i: (i % tps, 0)),
        pl.BlockSpec((tm, LANES), lambda i: (i % tps, 0)),
        pl.BlockSpec((ROT_DIM, tm), lambda i: (0, i % tps)),
        pl.BlockSpec((ROT_DIM, tm), lambda i: (0, i % tps)),
    ]
    return pl.pallas_call(
        kern, grid=(m // tm,), in_specs=in_specs, out_specs=out_specs, out_shape=out_shape,
        compiler_params=_params(("arbitrary",)), name="proj_prompt",
    )(x2, mod_p, g1, wqt, wkv, wkn, ca, sb, sc, ct, st)


def _diff_attn_kernel(qt_ref, k_ref, vt_ref, lam_ref, g_ref, o_ref, qz_sc, acc_sc, *, out_scale):
    tq = qt_ref.shape[-1]
    tk = vt_ref.shape[-1]
    h = pl.program_id(1)
    qi = pl.program_id(2)
    qz_sc[...] = jnp.zeros(qz_sc.shape, BF16)
    for g in range(2):
        for mp in range(2):
            src = g * 2 * DK_DIFF + mp * DK_DIFF
            dst = pl.multiple_of(h * 2 * DK_DIFF + mp * DK_DIFF, DK_DIFF)
            qz_sc[2 * g + mp, pl.ds(dst, DK_DIFF), :] = qt_ref[src:src + DK_DIFF, :]
    acc_sc[...] = jnp.zeros(acc_sc.shape, F32)

    def scores_of(ki):
        k = k_ref[pl.ds(pl.multiple_of(ki * tk, tk), tk), :]
        return tuple(_dot(k, qz_sc[gm]) for gm in range(4))

    def block(ki, masked, scores, stats):
        vt = vt_ref[ki]
        if masked:
            row = lax.broadcasted_iota(jnp.int32, (tk, tq), 0)
            col = lax.broadcasted_iota(jnp.int32, (tk, tq), 1)
            keep = row <= col
        ms, ls, ps, alphas = [], [], [], []
        for gm in range(4):
            s = scores[gm]
            if masked:
                s = jnp.where(keep, s, NEG)
            m_prev, l_prev = stats[0][gm], stats[1][gm]
            m_new = jnp.maximum(m_prev, jnp.max(s, axis=0, keepdims=True))
            alpha = jnp.exp2(m_prev - m_new)
            p = jnp.exp2(s - m_new)
            ms.append(m_new)
            ls.append(alpha * l_prev + jnp.sum(p, axis=0, keepdims=True))
            ps.append(p.astype(BF16))
            alphas.append(alpha)
        for gm in range(4):
            acc_sc[gm] = acc_sc[gm] * alphas[gm] + _dot(vt, ps[gm])
        return tuple(ms), tuple(ls)

    init = (tuple(jnp.full((1, tq), NEG, F32) for _ in range(4)),
            tuple(jnp.zeros((1, tq), F32) for _ in range(4)))

    def pair(k0, last_masked, stats):
        s0, s1 = scores_of(k0), scores_of(k0 + 1)
        return block(k0 + 1, last_masked, s1, block(k0, False, s0, stats))

    stats = lax.fori_loop(0, qi // 2, lambda j, st: pair(2 * j, False, st), init)
    _, l_fin = lax.cond(qi % 2 == 1,
                        lambda st: pair(qi - 1, True, st),
                        lambda st: block(qi, True, scores_of(qi), st), stats)

    lam = lam_ref[:, 0:1]
    outs = []
    for g in range(2):
        o = acc_sc[2 * g] / l_fin[2 * g] - lam * (acc_sc[2 * g + 1] / l_fin[2 * g + 1])
        ms = jnp.mean(o * o, axis=0, keepdims=True)
        outs.append((o * lax.rsqrt(ms + EPS)) * g_ref[...] * out_scale)
    o_ref[...] = jnp.concatenate(outs, axis=0).T.astype(o_ref.dtype)


def _diff_attn(layer, qt, kn, vdb, lam, subg, batch, seq, out_scale):
    tq = vdb.shape[-1]
    nk = vdb.shape[1]
    nq = seq // tq
    kern = functools.partial(_diff_attn_kernel, out_scale=out_scale)
    return pl.pallas_call(
        kern,
        grid=(batch, KV_DIFF, nq),
        in_specs=[
            pl.BlockSpec((None, 2 * 2 * DK_DIFF, tq), lambda b, h, i: (b, h, i)),
            pl.BlockSpec((seq, K_DIFF), lambda b, h, i: (b, 0)),
            pl.BlockSpec((None, nk, DV_DIFF, tq), lambda b, h, i: (b, 0, h, 0)),
            pl.BlockSpec((None, 1, LANES), lambda b, h, i: (layer, 0, 0)),
            pl.BlockSpec((None, DV_DIFF, 1), lambda b, h, i: (layer, 0, 0)),
        ],
        out_specs=pl.BlockSpec((tq, LANES), lambda b, h, i: (b * nq + i, h)),
        out_shape=jax.ShapeDtypeStruct((batch * seq, H_DIFF * DV_DIFF), BF16),
        scratch_shapes=[
            pltpu.VMEM((4, K_DIFF, tq), BF16),
            pltpu.VMEM((4, DV_DIFF, tq), F32),
        ],
        compiler_params=_params(("arbitrary", "arbitrary", "arbitrary")),
        name="diff_attn_prompt",
    )(qt, kn, vdb, lam, subg)


def _sb_attn_kernel(qt_ref, k_ref, vt_ref, u_ref, o_ref, qz_sc, acc_sc):
    tq = qt_ref.shape[-1]
    tk = vt_ref.shape[-1]
    nh = qz_sc.shape[0]
    qi = pl.program_id(2)
    qz_sc[...] = jnp.zeros(qz_sc.shape, BF16)
    for hh in range(nh):
        qz_sc[hh, hh * D_SB:(hh + 1) * D_SB, :] = qt_ref[hh * D_SB:(hh + 1) * D_SB, :]
    acc_sc[...] = jnp.zeros(acc_sc.shape, F32)

    def block(ki, masked, sticks):
        k = k_ref[pl.ds(pl.multiple_of(ki * tk, tk), tk), :]
        vt = vt_ref[ki]
        u = u_ref[...]
        if masked:
            row = lax.broadcasted_iota(jnp.int32, (tk, tq), 0)
            col = lax.broadcasted_iota(jnp.int32, (tk, tq), 1)
            keep = row < col
        zs = [_dot(k, qz_sc[hh]) for hh in range(nh)]
        parts = []
        for hh in range(nh):
            l1 = _log2_one_minus_beta(zs[hh])
            if masked:
                l1 = jnp.where(keep, l1, 0.0)
            l_hi = l1.astype(BF16)
            parts.append(jnp.concatenate([l_hi, (l1 - l_hi.astype(F32)).astype(BF16)], axis=0))
        sufs = [_dot(u, parts[hh]) for hh in range(nh)]
        weights, out = [], []
        for hh in range(nh):
            a = jnp.exp2(zs[hh] + sufs[hh] + sticks[hh])
            if masked:
                a = jnp.where(keep, a, 0.0)
            weights.append(a.astype(BF16))
            out.append(sticks[hh] + sufs[hh][0:1, :])
        for hh in range(nh):
            acc_sc[hh] += _dot(vt[hh * D_SB:(hh + 1) * D_SB, :], weights[hh])
        return tuple(out)

    def more(sticks):
        worst = sticks[0]
        for t in sticks[1:]:
            worst = jnp.maximum(worst, t)
        return jnp.max(worst) > -SB_STOP_LOG2

    sticks = block(qi, True, tuple(jnp.zeros((1, tq), F32) for _ in range(nh)))

    def cond(st):
        return jnp.logical_and(st[0] < qi, st[1])

    def body(st):
        new = block(qi - 1 - st[0], False, st[2])
        return st[0] + 1, more(new), new

    lax.while_loop(cond, body, (jnp.int32(0), more(sticks), sticks))
    o_ref[...] = jnp.concatenate([acc_sc[hh] for hh in range(nh)], axis=0).T.astype(o_ref.dtype)


def _sb_attn(qst, ksn, vsb, tri_t, batch, seq):
    tq = vsb.shape[-1]
    nk = vsb.shape[1]
    nq = seq // tq
    kl = MXU_DIM
    nh = kl // D_SB
    return pl.pallas_call(
        _sb_attn_kernel,
        grid=(batch, H_SB // nh, nq),
        in_specs=[
            pl.BlockSpec((None, kl, tq), lambda b, h, i: (b, h, i)),
            pl.BlockSpec((seq, kl), lambda b, h, i: (b, h)),
            pl.BlockSpec((None, nk, kl, tq), lambda b, h, i: (b, 0, h, 0)),
            pl.BlockSpec((tq, 2 * tq), lambda b, h, i: (0, 0)),
        ],
        out_specs=pl.BlockSpec((tq, kl), lambda b, h, i: (b * nq + i, h)),
        out_shape=jax.ShapeDtypeStruct((batch * seq, H_SB * D_SB), BF16),
        scratch_shapes=[
            pltpu.VMEM((nh, kl, tq), BF16),
            pltpu.VMEM((nh, D_SB, tq), F32),
        ],
        compiler_params=_params(("arbitrary", "arbitrary", "arbitrary")),
        name="sb_attn_prompt",
    )(qst, ksn, vsb, tri_t)


def _ffn_prompt_kernel(x_ref, od_ref, os_ref, mod_ref, g2_ref, wo_ref, wg_ref, wu_ref, wd_ref, fg_ref,
                       o_ref, *, d_model, final):
    d = d_model
    att = _dot(od_ref[...], wo_ref[0:Q_DIFF, :]) + _dot(os_ref[...], wo_ref[Q_DIFF:, :])
    x1 = x_ref[...] + mod_ref[:, 2 * d:3 * d] * att
    h = _norm_mod(x1, g2_ref[...], mod_ref[:, 4 * d:5 * d], mod_ref[:, 3 * d:4 * d]).astype(BF16)
    ff = wg_ref.shape[1]
    acc = None
    for c in range(ff // FF_CHUNK):
        sl = slice(c * FF_CHUNK, (c + 1) * FF_CHUNK)
        a = (_silu(_dot(h, wg_ref[:, sl])) * _dot(h, wu_ref[:, sl])).astype(BF16)
        part = _dot(a, wd_ref[sl, :])
        acc = part if acc is None else acc + part
    y = x1 + mod_ref[:, 5 * d:6 * d] * acc
    if final:
        y = _rmsnorm(y, fg_ref[...])
    o_ref[...] = y


def _ffn_prompt(layer, x2, od, osb, mod_p, g2, wo, wg, wu, wd, fg, seq, final):
    m, d = x2.shape
    tm = min(TOKEN_TILE, seq)
    tps = seq // tm
    ff = wg.shape[-1]
    once = pl.Buffered(1)
    kern = functools.partial(_ffn_prompt_kernel, d_model=d, final=final)
    return pl.pallas_call(
        kern,
        grid=(m // tm,),
        in_specs=[
            pl.BlockSpec((tm, d), lambda i: (i, 0)),
            pl.BlockSpec((tm, Q_DIFF), lambda i: (i, 0)),
            pl.BlockSpec((tm, QKV_SB), lambda i: (i, 0)),
            pl.BlockSpec((None, None, 1, 6 * d), lambda i: (layer, i // tps, 0, 0)),
            pl.BlockSpec((None, 1, d), lambda i: (layer, 0, 0)),
            pl.BlockSpec((None, d, d), lambda i: (layer, 0, 0), pipeline_mode=once),
            pl.BlockSpec((None, d, ff), lambda i: (layer, 0, 0), pipeline_mode=once),
            pl.BlockSpec((None, d, ff), lambda i: (layer, 0, 0), pipeline_mode=once),
            pl.BlockSpec((None, ff, d), lambda i: (layer, 0, 0), pipeline_mode=once),
            pl.BlockSpec((1, d), lambda i: (0, 0)),
        ],
        out_specs=pl.BlockSpec((tm, d), lambda i: (i, 0)),
        out_shape=jax.ShapeDtypeStruct((m, d), F32),
        compiler_params=_params(("arbitrary",)),
        name="ffn_prompt",
    )(x2, od, osb, mod_p, g2, wo, wg, wu, wd, fg)


def _proj_sample_kernel(x_ref, mod_ref, g_ref, wq_ref, wn_ref, wkv_ref, ca_ref, sb_ref, sc_ref, ct_ref, st_ref,
                        qd_ref, qs_ref, kn_ref, vn_ref, kd_ref, vd_ref, ks_ref, vs_ref,
                        *, d_model, q_scale, qs_scale):
    d = d_model
    h = _norm_mod(x_ref[...], g_ref[...], mod_ref[:, d:2 * d], mod_ref[:, 0:d]).astype(BF16)
    ca, sb, sc = ca_ref[...], sb_ref[...], sc_ref[...]
    rq = _dot(h, wq_ref[:, 0:Q_DIFF])
    for j in range(Q_DIFF // LANES):
        qd_ref[:, j * LANES:(j + 1) * LANES] = _rope_lanes(rq[:, j * LANES:(j + 1) * LANES], ca, sb, sc) * q_scale
    qs_ref[...] = _dot(h, wq_ref[:, Q_DIFF:Q_DIFF + QKV_SB]) * qs_scale
    rk = _dot(h, wn_ref[:, 0:K_DIFF])
    for j in range(K_DIFF // LANES):
        kn_ref[:, j * LANES:(j + 1) * LANES] = _rope_lanes(rk[:, j * LANES:(j + 1) * LANES], ca, sb, sc)
    vn_ref[...] = _dot(h, wn_ref[:, K_DIFF:K_DIFF + V_DIFF])
    kd, vd, ks, vs = _split_kv_rows(h, wkv_ref, ct_ref[...], st_ref[...])
    kd_ref[...] = kd
    vd_ref[...] = vd
    ks_ref[...] = ks
    vs_ref[...] = vs


def _proj_sample(layer, xs, mod_all, g1, wq_s, wn, wkv, rope_l, rope_t):
    ns, d = xs.shape
    ca, sb, sc = rope_l
    ct, st = rope_t
    kern = functools.partial(_proj_sample_kernel, d_model=d,
                             q_scale=DK_DIFF ** -0.5 * LOG2E, qs_scale=D_SB ** -0.5 * LOG2E)
    full = lambda shape: pl.BlockSpec(shape, lambda i: (0,) * len(shape))
    out_shape = [
        jax.ShapeDtypeStruct((ns, Q_DIFF), F32),
        jax.ShapeDtypeStruct((ns, QKV_SB), F32),
        jax.ShapeDtypeStruct((ns, K_DIFF), F32),
        jax.ShapeDtypeStruct((ns, V_DIFF), F32),
        jax.ShapeDtypeStruct((K_DIFF, ns), F32),
        jax.ShapeDtypeStruct((V_DIFF, ns), F32),
        jax.ShapeDtypeStruct((QKV_SB, ns), F32),
        jax.ShapeDtypeStruct((QKV_SB, ns), F32),
    ]
    return pl.pallas_call(
        kern,
        grid=(1,),
        in_specs=[
            full((ns, d)),
            pl.BlockSpec((None, ns, 6 * d), lambda i: (layer, 0, 0)),
            pl.BlockSpec((None, 1, d), lambda i: (layer, 0, 0)),
            pl.BlockSpec((None, d, Q_DIFF + QKV_SB), lambda i: (layer, 0, 0)),
            pl.BlockSpec((None, d, K_DIFF + V_DIFF), lambda i: (layer, 0, 0)),
            pl.BlockSpec((None, KV_ROWS, d), lambda i: (layer, 0, 0)),
            full((1, LANES)), full((1, LANES)), full((1, LANES)),
            full((ROT_DIM, 1)), full((ROT_DIM, 1)),
        ],
        out_specs=[full(s.shape) for s in out_shape],
        out_shape=out_shape,
        compiler_params=_params(("arbitrary",)),
        name="proj_sample",
    )(xs, mod_all, g1, wq_s, wn, wkv, ca, sb, sc, ct, st)


def _suffix_sum_lanes(x):
    rows, n = x.shape
    lane = lax.broadcasted_iota(jnp.int32, (rows, LANES), 1)
    tiles = []
    for t in range(n // LANES):
        y = x[:, t * LANES:(t + 1) * LANES]
        sh = 1
        while sh < LANES:
            y = y + jnp.where(lane < LANES - sh, pltpu.roll(y, LANES - sh, 1), 0.0)
            sh *= 2
        tiles.append(y)
    carry = jnp.zeros((rows, 1), F32)
    out = [None] * len(tiles)
    for t in range(len(tiles) - 1, -1, -1):
        out[t] = tiles[t] + carry
        carry = carry + tiles[t][:, 0:1]
    return jnp.concatenate(out, axis=1)


def _decode_attn_kernel(pt_ref, qd_ref, qs_ref, kn_ref, vn_ref, lam_ref, ckd, cvd, cks, cvs,
                        od_ref, os_ref, kd_buf, vd_buf, kt_buf, vt_buf, kr_buf, vr_buf, sem, sem_r,
                        *, layer, n_pages, page):
    i = pl.program_id(0)
    n = pl.num_programs(0)
    slot = lax.rem(i, 2)

    def copies(seq, sl):
        out = []
        for j in range(n_pages):
            pg = pt_ref[seq, j]
            dst = pl.ds(j * page, page)
            out.append(pltpu.make_async_copy(ckd.at[layer, pg], kd_buf.at[sl, :, dst], sem.at[sl, 0]))
            out.append(pltpu.make_async_copy(cvd.at[layer, pg], vd_buf.at[sl, :, dst], sem.at[sl, 1]))
        pg = pt_ref[seq, n_pages - 1]
        out.append(pltpu.make_async_copy(cks.at[layer, pg], kt_buf.at[sl], sem.at[sl, 2]))
        out.append(pltpu.make_async_copy(cvs.at[layer, pg], vt_buf.at[sl], sem.at[sl, 3]))
        return out

    def rest_copies(seq):
        out = []
        for j in range(n_pages - 1):
            pg = pt_ref[seq, j]
            dst = pl.ds(j * page, page)
            out.append(pltpu.make_async_copy(cks.at[layer, pg], kr_buf.at[:, dst], sem_r.at[0]))
            out.append(pltpu.make_async_copy(cvs.at[layer, pg], vr_buf.at[:, dst], sem_r.at[1]))
        return out

    @pl.when(i == 0)
    def _():
        for cp in copies(0, 0):
            cp.start()

    @pl.when(i + 1 < n)
    def _():
        for cp in copies(i + 1, 1 - slot):
            cp.start()

    for cp in copies(i, slot):
        cp.wait()

    rs = lax.broadcasted_iota(jnp.int32, (H_SB, QKV_SB), 0)
    cs = lax.broadcasted_iota(jnp.int32, (H_SB, QKV_SB), 1)
    hsel = (cs >> 6) == rs
    qsm = jnp.where(hsel, jnp.broadcast_to(qs_ref[pl.ds(i, 1), :], (H_SB, QKV_SB)), 0.0)
    z = _dot(qsm, kt_buf[slot])

    r8 = lax.broadcasted_iota(jnp.int32, (SUBLANES, K_DIFF), 0)
    c8 = lax.broadcasted_iota(jnp.int32, (SUBLANES, K_DIFF), 1)
    sel = (c8 >> 5) == r8
    qrow = qd_ref[pl.ds(i, 1), :]
    qm = jnp.concatenate(
        [jnp.where(sel, jnp.broadcast_to(qrow[:, g * K_DIFF:(g + 1) * K_DIFF], (SUBLANES, K_DIFF)), 0.0)
         for g in range(2)], axis=0)
    s = _dot(qm, kd_buf[slot])
    kn = kn_ref[pl.ds(i, 1), :]
    s_self = jnp.sum(qm * kn, axis=1, keepdims=True)
    mx = jnp.maximum(jnp.max(s, axis=1, keepdims=True), s_self)
    p = jnp.exp2(s - mx)
    p_self = jnp.exp2(s_self - mx)
    den = jnp.sum(p, axis=1, keepdims=True) + p_self
    o = _dot_nt(p, vd_buf[slot]) + p_self * vn_ref[pl.ds(i, 1), :]
    r16 = lax.broadcasted_iota(jnp.int32, (2 * SUBLANES, V_DIFF), 0)
    c16 = lax.broadcasted_iota(jnp.int32, (2 * SUBLANES, V_DIFF), 1)
    lam = lam_ref[:, 0:1]
    coef = jnp.where((r16[:, 0:1] & 1) == 1, -lam, 1.0) / den
    w = jnp.where((c16 >> 6) == ((r16 & 7) >> 1), o * coef, 0.0)
    od_ref[pl.ds(i, 1), :] = jnp.concatenate(
        [jnp.sum(w[g * SUBLANES:(g + 1) * SUBLANES], axis=0, keepdims=True) for g in range(2)], axis=1)

    suf = _suffix_sum_lanes(_log2_one_minus_beta(z))
    o_new = _dot_nt(jnp.exp2(z + suf), vt_buf[slot])
    stick = suf[:, 0:1]
    os_ref[pl.ds(i, 1), :] = jnp.sum(jnp.where(hsel, o_new, 0.0), axis=0, keepdims=True)

    @pl.when(jnp.max(stick) > -SB_STOP_LOG2)
    def _():
        for cp in rest_copies(i):
            cp.start()
        for cp in rest_copies(i):
            cp.wait()
        z_r = _dot(qsm, kr_buf[...])
        suf_r = _suffix_sum_lanes(_log2_one_minus_beta(z_r))
        o_all = o_new + _dot_nt(jnp.exp2(z_r + suf_r + stick), vr_buf[...])
        os_ref[pl.ds(i, 1), :] = jnp.sum(jnp.where(hsel, o_all, 0.0), axis=0, keepdims=True)


def _decode_attn(layer, page_table, qd, qs, kn, vn, lam, ckd, cvd, cks, cvs):
    ns, n_pages = page_table.shape
    page = ckd.shape[-1]
    past = n_pages * page
    kern = functools.partial(_decode_attn_kernel, layer=layer, n_pages=n_pages, page=page)
    res = lambda shape: pl.BlockSpec(shape, lambda i, pt: (0,) * len(shape))
    any_spec = pl.BlockSpec(memory_space=pl.ANY)
    grid_spec = pltpu.PrefetchScalarGridSpec(
        num_scalar_prefetch=1,
        grid=(ns,),
        in_specs=[
            res((ns, Q_DIFF)), res((ns, QKV_SB)), res((ns, K_DIFF)), res((ns, V_DIFF)),
            pl.BlockSpec((None, 1, LANES), lambda i, pt: (layer, 0, 0)),
            any_spec, any_spec, any_spec, any_spec,
        ],
        out_specs=[res((ns, H_DIFF * DV_DIFF)), res((ns, QKV_SB))],
        scratch_shapes=[
            pltpu.VMEM((2, K_DIFF, past), F32),
            pltpu.VMEM((2, V_DIFF, past), F32),
            pltpu.VMEM((2, QKV_SB, page), F32),
            pltpu.VMEM((2, QKV_SB, page), F32),
            pltpu.VMEM((QKV_SB, past - page), F32),
            pltpu.VMEM((QKV_SB, past - page), F32),
            pltpu.SemaphoreType.DMA((2, 4)),
            pltpu.SemaphoreType.DMA((2,)),
        ],
    )
    return pl.pallas_call(
        kern,
        grid_spec=grid_spec,
        out_shape=[jax.ShapeDtypeStruct((ns, H_DIFF * DV_DIFF), F32), jax.ShapeDtypeStruct((ns, QKV_SB), F32)],
        compiler_params=_params(("arbitrary",)),
        name="decode_attn",
    )(page_table, qd, qs, kn, vn, lam, ckd, cvd, cks, cvs)


def _ffn_sample_kernel(x_ref, od_ref, os_ref, mod_ref, g2_ref, gs_ref, seg_ref, wo_ref, wg_ref, wu_ref, wd_ref,
                       fg_ref, o_ref, x1_sc, h_sc, acc_sc, *, d_model, out_scale, final):
    d = d_model
    c = pl.program_id(0)

    @pl.when(c == 0)
    def _():
        o = od_ref[...]
        ss = o * o
        s_hi = ss.astype(BF16)
        s_lo = (ss - s_hi.astype(F32)).astype(BF16)
        seg = _dot(s_hi, seg_ref[...]) + _dot(s_lo, seg_ref[...])
        y = (o * lax.rsqrt(seg * (1.0 / DV_DIFF) + EPS)) * gs_ref[...] * out_scale
        att = _dot(y.astype(BF16), wo_ref[0:Q_DIFF, :]) + _dot(os_ref[...].astype(BF16), wo_ref[Q_DIFF:, :])
        x1 = x_ref[...] + mod_ref[:, 2 * d:3 * d] * att
        x1_sc[...] = x1
        h_sc[...] = _norm_mod(x1, g2_ref[...], mod_ref[:, 4 * d:5 * d], mod_ref[:, 3 * d:4 * d]).astype(BF16)
        acc_sc[...] = jnp.zeros(acc_sc.shape, F32)

    h = h_sc[...]
    a = (_silu(_dot(h, wg_ref[...])) * _dot(h, wu_ref[...])).astype(BF16)
    acc_sc[...] += _dot(a, wd_ref[...])

    @pl.when(c == pl.num_programs(0) - 1)
    def _():
        y = x1_sc[...] + mod_ref[:, 5 * d:6 * d] * acc_sc[...]
        if final:
            y = _rmsnorm(y, fg_ref[...])
        o_ref[...] = y


def _ffn_sample(layer, xs, od, osb, mod_all, g2, subg8, seg, wo_s, wg, wu, wd, fg, out_scale, final):
    ns, d = xs.shape
    ff = wg.shape[-1]
    kern = functools.partial(_ffn_sample_kernel, d_model=d, out_scale=out_scale, final=final)
    full = lambda shape: pl.BlockSpec(shape, lambda c: (0,) * len(shape))
    return pl.pallas_call(
        kern,
        grid=(ff // FF_CHUNK,),
        in_specs=[
            full((ns, d)), full((ns, Q_DIFF)), full((ns, QKV_SB)),
            pl.BlockSpec((None, ns, 6 * d), lambda c: (layer, 0, 0)),
            pl.BlockSpec((None, 1, d), lambda c: (layer, 0, 0)),
            pl.BlockSpec((None, 1, Q_DIFF), lambda c: (layer, 0, 0)),
            full((Q_DIFF, Q_DIFF)),
            pl.BlockSpec((None, d, d), lambda c: (layer, 0, 0)),
            pl.BlockSpec((None, d, FF_CHUNK), lambda c: (layer, 0, c)),
            pl.BlockSpec((None, d, FF_CHUNK), lambda c: (layer, 0, c)),
            pl.BlockSpec((None, FF_CHUNK, d), lambda c: (layer, c, 0)),
            full((1, d)),
        ],
        out_specs=full((ns, d)),
        out_shape=jax.ShapeDtypeStruct((ns, d), F32),
        scratch_shapes=[pltpu.VMEM((ns, d), F32), pltpu.VMEM((ns, d), BF16), pltpu.VMEM((ns, d), F32)],
        compiler_params=_params(("arbitrary",)),
        name="ffn_sample",
    )(xs, od, osb, mod_all, g2, subg8, seg, wo_s, wg, wu, wd, fg)


def _rope_tables(pos):
    half = ROT_DIM // 2
    inv = ROPE_THETA ** (-jnp.arange(0, ROT_DIM, 2, dtype=F32) / ROT_DIM)
    ang = pos.astype(F32)[:, None] * inv[None, :]
    cos, sin = jnp.cos(ang), jnp.sin(ang)
    dl = np.arange(LANES) % DK_DIFF
    jl = dl % half
    ca = jnp.where(dl < ROT_DIM, cos[:, jl], 1.0)
    sb = jnp.where(dl < half, -sin[:, jl], 0.0)
    sc = jnp.where((dl >= half) & (dl < ROT_DIM), sin[:, jl], 0.0)
    dr = np.arange(ROT_DIM)
    ct = cos[:, dr % half].T
    st = jnp.where((dr < half)[:, None], -sin[:, dr % half].T, sin[:, dr % half].T)
    return (ca, sb, sc), (ct, st)


def kernel(x_prompt, x_sample, c_prompt, c_sample, cache_diff_k, cache_diff_v, cache_sb_k, cache_sb_v,
           page_table, norm1_g, norm2_g, w_ada, b_ada, w_in, w_o, lam_q1, lam_k1, lam_q2, lam_k2,
           subln_g, w_gate, w_up, w_down, final_g):
    batch, seq, d = x_prompt.shape
    ns = x_sample.shape[0]
    depth = w_in.shape[0]
    n_pool, page = cache_diff_k.shape[1], cache_diff_k.shape[2]
    n_pages = page_table.shape[1]
    past = n_pages * page
    assert x_sample.shape[1] == 1 and seq % min(ATT_BLOCK, seq) == 0

    c0, c1, c2, c3, c4 = Q_DIFF, Q_DIFF + K_DIFF, Q_DIFF + K_DIFF + V_DIFF, \
        Q_DIFF + K_DIFF + V_DIFF + QKV_SB, Q_DIFF + K_DIFF + V_DIFF + 2 * QKV_SB
    w_in_b = w_in.astype(BF16)
    perm = np.array([128 * h + 64 * g + e for g in range(2) for h in range(KV_DIFF) for e in range(64)])
    wq_s = jnp.concatenate([w_in_b[:, :, :c0][:, :, perm], w_in_b[:, :, c2:c3]], axis=2)
    wn = w_in_b[:, :, c0:c2]
    wkn = jnp.concatenate([w_in_b[:, :, c0:c1], w_in_b[:, :, c3:c4]], axis=2)
    w_in_t = jnp.swapaxes(w_in_b, 1, 2)
    wqt = jnp.concatenate([w_in_t[:, :c0], w_in_t[:, c2:c3]], axis=1)
    wkv = jnp.concatenate([w_in_t[:, c0:c2], w_in_t[:, c3:]], axis=1)
    wo = w_o.astype(BF16)
    wo_s = jnp.concatenate([wo[:, :Q_DIFF][:, perm], wo[:, Q_DIFF:]], axis=1)
    wg, wu, wd = w_gate.astype(BF16), w_up.astype(BF16), w_down.astype(BF16)

    g1 = norm1_g.reshape(depth, 1, d)
    g2 = norm2_g.reshape(depth, 1, d)
    fg = final_g.reshape(1, d)
    subg = subln_g.reshape(depth, DV_DIFF, 1)
    subg8 = jnp.tile(subln_g, (1, H_DIFF)).reshape(depth, 1, H_DIFF * DV_DIFF)
    lam_inits = [0.8 - 0.6 * math.exp(-0.3 * l) for l in range(depth)]
    lam = _lambdas(lam_q1, lam_k1, lam_q2, lam_k2, jnp.asarray(lam_inits, F32).reshape(depth, 1))
    lam = lam.reshape(depth, 1, LANES)

    mod_all = _modulation(jnp.concatenate([c_sample, c_prompt], axis=0), w_ada, b_ada)
    mod_p = mod_all[:, ns:].reshape(depth, batch, 1, 6 * d)

    rope_pl, rope_pt = _rope_tables(jnp.arange(seq, dtype=jnp.int32))
    rope_sl, rope_st = _rope_tables(past + jnp.arange(1, dtype=jnp.int32))

    tk = min(ATT_BLOCK, seq)
    tri_t = (np.arange(tk)[None, :] >= np.arange(tk)[:, None]).astype(np.float32)
    tri_t = jnp.asarray(np.concatenate([tri_t, tri_t], axis=1), BF16)
    segm = (np.arange(Q_DIFF)[:, None] // DV_DIFF == np.arange(Q_DIFF)[None, :] // DV_DIFF).astype(np.float32)
    segm = jnp.asarray(segm, BF16)

    def cache_rows(c):
        return jnp.transpose(c, (0, 1, 3, 4, 2)).reshape(depth, n_pool, c.shape[3] * c.shape[4], page)

    ckd, cvd, cks, cvs = (cache_rows(c) for c in (cache_diff_k, cache_diff_v, cache_sb_k, cache_sb_v))

    xp = x_prompt.reshape(batch * seq, d)
    xs = x_sample.reshape(ns, d)
    rows_p, rows_s = [], []
    for l in range(depth):
        out_scale = 1.0 - lam_inits[l]
        final = l == depth - 1
        (qt, qst, kn, ksn, kd, vd, ks, vs, vdb, vsb) = _proj_prompt(
            l, xp, mod_p, g1, wqt, wkv, wkn, rope_pl, rope_pt, batch, seq)
        od = _diff_attn(l, qt, kn, vdb, lam, subg, batch, seq, out_scale)
        osb = _sb_attn(qst, ksn, vsb, tri_t, batch, seq)
        xp = _ffn_prompt(l, xp, od, osb, mod_p, g2, wo, wg, wu, wd, fg, seq, final)
        rows_p.append((kd, vd, ks, vs))

        (qd_s, qs_s, kn_s, vn_s, kd_s, vd_s, ks_s, vs_s) = _proj_sample(
            l, xs, mod_all, g1, wq_s, wn, wkv, rope_sl, rope_st)
        od_s, os_s = _decode_attn(l, page_table, qd_s, qs_s, kn_s, vn_s, lam, ckd, cvd, cks, cvs)
        xs = _ffn_sample(l, xs, od_s, os_s, mod_all, g2, subg8, segm, wo_s, wg, wu, wd, fg, out_scale, final)
        rows_s.append((kd_s, vd_s, ks_s, vs_s))

    def prompt_rows(idx, heads):
        a = jnp.stack([r[idx] for r in rows_p])
        return jnp.transpose(a.reshape(depth, batch, heads, HEAD_DIM, seq), (0, 1, 4, 2, 3))

    def sample_rows(idx, heads):
        a = jnp.stack([r[idx] for r in rows_s])
        return jnp.transpose(a.reshape(depth, heads, HEAD_DIM, ns), (0, 3, 1, 2))[:, :, None]

    return (xp.reshape(batch, seq, d), xs.reshape(ns, 1, d),
            prompt_rows(0, KV_DIFF), prompt_rows(1, KV_DIFF), prompt_rows(2, H_SB), prompt_rows(3, H_SB),
            sample_rows(0, KV_DIFF), sample_rows(1, KV_DIFF), sample_rows(2, H_SB), sample_rows(3, H_SB))
```

```python
import functools
import math

import jax
import jax.numpy as jnp
import numpy as np
from jax import lax
from jax.experimental import pallas as pl
from jax.experimental.pallas import tpu as pltpu

F32 = jnp.float32
BF16 = jnp.bfloat16

HEAD_DIM = 64
H_DIFF = 8
KV_DIFF = 4
DK_DIFF = HEAD_DIM // 2
DV_DIFF = HEAD_DIM
H_SB = 8
D_SB = HEAD_DIM
ROT_DIM = DK_DIFF // 4
ROPE_THETA = 500000.0
EPS = 1e-6
NEG = -1e30
LOG2E = 1.4426950408889634
Q_DIFF = H_DIFF * 2 * DK_DIFF
K_DIFF = KV_DIFF * 2 * DK_DIFF
V_DIFF = KV_DIFF * DV_DIFF
QKV_SB = H_SB * D_SB
KV_ROWS = K_DIFF + V_DIFF + 2 * QKV_SB

LANES = 128
SUBLANES = 8
MXU_DIM = 256
VMEM_LIMIT = 56 * 1024 * 1024

ATT_BLOCK = 256
TOKEN_TILE = 512
FF_CHUNK = 256
DECODE_SLOTS = 3
SB_STOP_LOG2 = 100.0


def _params(sem, vmem=VMEM_LIMIT):
    return pltpu.CompilerParams(dimension_semantics=sem, vmem_limit_bytes=vmem)


def _dot(a, b):
    return jnp.dot(a, b, preferred_element_type=F32)


def _dot_nt(a, b):
    return lax.dot_general(a, b, (((1,), (1,)), ((), ())), preferred_element_type=F32)


def _silu(x):
    return x * (1.0 / (1.0 + jnp.exp(-x)))


def _norm_mod(x, g, scale, shift):
    ms = jnp.mean(x * x, axis=-1, keepdims=True)
    return (x * lax.rsqrt(ms + EPS)) * g * (1.0 + scale) + shift


def _rmsnorm(x, g):
    ms = jnp.mean(x * x, axis=-1, keepdims=True)
    return (x * lax.rsqrt(ms + EPS)) * g


def _log2_one_minus_beta(z):
    return jnp.log(1.0 + jnp.exp2(-jnp.abs(z))) * (-LOG2E) - jnp.maximum(z, 0.0)


def _rope_lanes(t, ca, sb, sc):
    return t * ca + pltpu.roll(t, LANES - ROT_DIM // 2, 1) * sb + pltpu.roll(t, ROT_DIM // 2, 1) * sc


def _rope_rows(slab, cos_t, sin_t):
    return slab * cos_t + pltpu.roll(slab, ROT_DIM // 2, 0) * sin_t


def _mod_kernel(c_ref, w_ref, b_ref, o_ref):
    s = _silu(c_ref[...]).astype(BF16)
    o_ref[...] = _dot(s, w_ref[...].astype(BF16)) + b_ref[...]


def _modulation(c_all, w_ada, b_ada):
    depth, d, n = w_ada.shape
    rows = c_all.shape[0]
    tn = 1536
    return pl.pallas_call(
        _mod_kernel,
        grid=(depth, n // tn),
        in_specs=[
            pl.BlockSpec((rows, d), lambda l, j: (0, 0)),
            pl.BlockSpec((None, d, tn), lambda l, j: (l, 0, j)),
            pl.BlockSpec((None, 1, tn), lambda l, j: (l, 0, j)),
        ],
        out_specs=pl.BlockSpec((None, rows, tn), lambda l, j: (l, 0, j)),
        out_shape=jax.ShapeDtypeStruct((depth, rows, n), F32),
        compiler_params=_params(("arbitrary", "arbitrary")),
        name="adaln_mod",
    )(c_all, w_ada, b_ada.reshape(depth, 1, n))


def _lam_kernel(q1_ref, k1_ref, q2_ref, k2_ref, li_ref, o_ref):
    a = jnp.sum(q1_ref[...] * k1_ref[...], axis=-1, keepdims=True)
    b = jnp.sum(q2_ref[...] * k2_ref[...], axis=-1, keepdims=True)
    lam = jnp.exp(a) - jnp.exp(b) + li_ref[...]
    o_ref[...] = jnp.broadcast_to(lam, o_ref.shape)


def _lambdas(lam_q1, lam_k1, lam_q2, lam_k2, lam_init):
    depth = lam_q1.shape[0]
    return pl.pallas_call(
        _lam_kernel,
        out_shape=jax.ShapeDtypeStruct((depth, LANES), F32),
        name="diff_lambda",
    )(lam_q1, lam_k1, lam_q2, lam_k2, lam_init)


def _rope_all_rows(x, cos_t, sin_t):
    parts = []
    for g in range(x.shape[0] // DK_DIFF):
        r0 = g * DK_DIFF
        parts.append(_rope_rows(x[r0:r0 + ROT_DIM, :], cos_t, sin_t))
        parts.append(x[r0 + ROT_DIM:r0 + DK_DIFF, :])
    return jnp.concatenate(parts, axis=0)


def _split_kv_rows(h, wkv_ref, cos_t, sin_t):
    kd = _rope_all_rows(_dot_nt(wkv_ref[0:K_DIFF, :], h), cos_t, sin_t)
    vd = _dot_nt(wkv_ref[K_DIFF:K_DIFF + V_DIFF, :], h)
    ks = _dot_nt(wkv_ref[K_DIFF + V_DIFF:K_DIFF + V_DIFF + QKV_SB, :], h)
    vs = _dot_nt(wkv_ref[K_DIFF + V_DIFF + QKV_SB:KV_ROWS, :], h)
    return kd, vd, ks, vs


def _proj_prompt_kernel(x_ref, mod_ref, g_ref, wqt_ref, wkv_ref, wkn_ref, ca_ref, sb_ref, sc_ref, ct_ref, st_ref,
                        qt_ref, qst_ref, kn_ref, ksn_ref, kd_ref, vd_ref, ks_ref, vs_ref, vdb_ref, vsb_ref,
                        *, d_model, q_scale, qs_scale):
    d = d_model
    h = _norm_mod(x_ref[...], g_ref[...], mod_ref[:, d:2 * d], mod_ref[:, 0:d]).astype(BF16)
    ct, st = ct_ref[...], st_ref[...]
    qt = _rope_all_rows(_dot_nt(wqt_ref[0:Q_DIFF, :], h), ct, st)
    qt_ref[...] = (qt * q_scale).astype(BF16)
    qst_ref[...] = (_dot_nt(wqt_ref[Q_DIFF:Q_DIFF + QKV_SB, :], h) * qs_scale).astype(BF16)
    ca, sb, sc = ca_ref[...], sb_ref[...], sc_ref[...]
    rk = _dot(h, wkn_ref[:, 0:K_DIFF])
    for j in range(K_DIFF // LANES):
        sl = slice(j * LANES, (j + 1) * LANES)
        kn_ref[:, sl] = _rope_lanes(rk[:, sl], ca, sb, sc).astype(BF16)
    ksn_ref[...] = _dot(h, wkn_ref[:, K_DIFF:K_DIFF + QKV_SB]).astype(BF16)
    kd, vd, ks, vs = _split_kv_rows(h, wkv_ref, ct, st)
    kd_ref[...] = kd
    vd_ref[...] = vd
    ks_ref[...] = ks
    vs_ref[...] = vs
    nblk = vdb_ref.shape[0]
    tk = vdb_ref.shape[-1]
    for j in range(nblk):
        sl = slice(j * tk, (j + 1) * tk)
        vdb_ref[j] = vd[:, sl].astype(BF16)
        vsb_ref[j] = vs[:, sl].astype(BF16)


def _proj_prompt(layer, x2, mod_p, g1, wqt, wkv, wkn, rope_l, rope_t, batch, seq):
    m, d = x2.shape
    tm = min(TOKEN_TILE, seq)
    tk = min(ATT_BLOCK, seq)
    tps = seq // tm
    nblk = tm // tk
    nk = seq // tk
    ca, sb, sc = rope_l
    ct, st = rope_t
    kern = functools.partial(_proj_prompt_kernel, d_model=d,
                             q_scale=DK_DIFF ** -0.5 * LOG2E, qs_scale=D_SB ** -0.5 * LOG2E)
    tok = lambda i: (i // tps, 0, i % tps)
    blk = lambda i: (i // tps, i % tps, 0, 0)
    out_shape = [
        jax.ShapeDtypeStruct((batch, Q_DIFF, seq), BF16),
        jax.ShapeDtypeStruct((batch, QKV_SB, seq), BF16),
        jax.ShapeDtypeStruct((m, K_DIFF), BF16),
        jax.ShapeDtypeStruct((m, QKV_SB), BF16),
        jax.ShapeDtypeStruct((batch, K_DIFF, seq), F32),
        jax.ShapeDtypeStruct((batch, V_DIFF, seq), F32),
        jax.ShapeDtypeStruct((batch, QKV_SB, seq), F32),
        jax.ShapeDtypeStruct((batch, QKV_SB, seq), F32),
        jax.ShapeDtypeStruct((batch, nk, V_DIFF, tk), BF16),
        jax.ShapeDtypeStruct((batch, nk, QKV_SB, tk), BF16),
    ]
    out_specs = [
        pl.BlockSpec((None, Q_DIFF, tm), tok),
        pl.BlockSpec((None, QKV_SB, tm), tok),
        pl.BlockSpec((tm, K_DIFF), lambda i: (i, 0)),
        pl.BlockSpec((tm, QKV_SB), lambda i: (i, 0)),
        pl.BlockSpec((None, K_DIFF, tm), tok),
        pl.BlockSpec((None, V_DIFF, tm), tok),
        pl.BlockSpec((None, QKV_SB, tm), tok),
        pl.BlockSpec((None, QKV_SB, tm), tok),
        pl.BlockSpec((None, nblk, V_DIFF, tk), blk),
        pl.BlockSpec((None, nblk, QKV_SB, tk), blk),
    ]
    in_specs = [
        pl.BlockSpec((tm, d), lambda i: (i, 0)),
        pl.BlockSpec((None, None, 1, 6 * d), lambda i: (layer, i // tps, 0, 0)),
        pl.BlockSpec((None, 1, d), lambda i: (layer, 0, 0)),
        pl.BlockSpec((None, Q_DIFF + QKV_SB, d), lambda i: (layer, 0, 0)),
        pl.BlockSpec((None, KV_ROWS, d), lambda i: (layer, 0, 0)),
        pl.BlockSpec((None, d, K_DIFF + QKV_SB), lambda i: (layer, 0, 0)),
        pl.BlockSpec((tm, LANES), lambda i: (i % tps, 0)),
        pl.BlockSpec((tm, LANES), lambda i: (i % tps, 0)),
        pl.BlockSpec((tm, LANES), lambda i: (i % tps, 0)),
        pl.BlockSpec((ROT_DIM, tm), lambda i: (0, i % tps)),
        pl.BlockSpec((ROT_DIM, tm), lambda i: (0, i % tps)),
    ]
    return pl.pallas_call(
        kern, grid=(m // tm,), in_specs=in_specs, out_specs=out_specs, out_shape=out_shape,
        compiler_params=_params(("arbitrary",)), name="proj_prompt",
    )(x2, mod_p, g1, wqt, wkv, wkn, ca, sb, sc, ct, st)


def _diff_attn_kernel(qt_ref, k_ref, vt_ref, lam_ref, g_ref, o_ref, qz_sc, acc_sc, sa_sc, sb_sc, *, out_scale):
    tq = qt_ref.shape[-1]
    tk = vt_ref.shape[-1]
    h = pl.program_id(1)
    qi = pl.program_id(2)
    @pl.when(qi == 0)
    def _():
        qz_sc[...] = jnp.zeros(qz_sc.shape, BF16)

    for g in range(2):
        for mp in range(2):
            src = g * 2 * DK_DIFF + mp * DK_DIFF
            dst = pl.multiple_of(h * 2 * DK_DIFF + mp * DK_DIFF, DK_DIFF)
            qz_sc[2 * g + mp, pl.ds(dst, DK_DIFF), :] = qt_ref[src:src + DK_DIFF, :]
    acc_sc[...] = jnp.zeros(acc_sc.shape, F32)
    ones = jnp.ones((acc_sc.shape[1] - DV_DIFF, tk), BF16)

    def scores_to(dst, ki):
        k = k_ref[pl.ds(pl.multiple_of(ki * tk, tk), tk), :]
        for gm in range(4):
            dst[gm] = _dot(k, qz_sc[gm])

    def block(ki, masked, src, ms):
        vt = jnp.concatenate([vt_ref[ki], ones], axis=0)
        if masked:
            row = lax.broadcasted_iota(jnp.int32, (tk, tq), 0)
            col = lax.broadcasted_iota(jnp.int32, (tk, tq), 1)
            keep = row <= col
        new_ms, ps, alphas = [], [], []
        for gm in range(4):
            s = src[gm]
            if masked:
                s = jnp.where(keep, s, NEG)
            m_new = jnp.maximum(ms[gm], jnp.max(s, axis=0, keepdims=True))
            alphas.append(jnp.exp2(ms[gm] - m_new))
            ps.append(jnp.exp2(s - m_new).astype(BF16))
            new_ms.append(m_new)
        for gm in range(4):
            acc_sc[gm] = acc_sc[gm] * alphas[gm] + _dot(vt, ps[gm])
        return tuple(new_ms)

    scores_to(sa_sc, 0)

    def body(j, ms):
        scores_to(sb_sc, 2 * j + 1)
        ms = block(2 * j, False, sa_sc, ms)
        scores_to(sa_sc, 2 * j + 2)
        return block(2 * j + 1, False, sb_sc, ms)

    ms = lax.fori_loop(0, qi // 2, body, tuple(jnp.full((1, tq), NEG, F32) for _ in range(4)))

    @pl.when(qi % 2 == 1)
    def _():
        scores_to(sb_sc, qi)
        block(qi, True, sb_sc, block(qi - 1, False, sa_sc, ms))

    @pl.when(qi % 2 == 0)
    def _():
        block(qi, True, sa_sc, ms)

    lam = lam_ref[:, 0:1]
    outs = []
    for g in range(2):
        a0, a1 = acc_sc[2 * g], acc_sc[2 * g + 1]
        o = (a0[0:DV_DIFF] / a0[DV_DIFF:DV_DIFF + 1] - lam * (a1[0:DV_DIFF] / a1[DV_DIFF:DV_DIFF + 1]))
        ms2 = jnp.mean(o * o, axis=0, keepdims=True)
        outs.append((o * lax.rsqrt(ms2 + EPS)) * g_ref[...] * out_scale)
    o_ref[...] = jnp.concatenate(outs, axis=0).T.astype(o_ref.dtype)


def _diff_attn(layer, qt, kn, vdb, lam, subg, batch, seq, out_scale):
    tq = vdb.shape[-1]
    nk = vdb.shape[1]
    nq = seq // tq
    kern = functools.partial(_diff_attn_kernel, out_scale=out_scale)
    return pl.pallas_call(
        kern,
        grid=(batch, KV_DIFF, nq),
        in_specs=[
            pl.BlockSpec((None, 2 * 2 * DK_DIFF, tq), lambda b, h, i: (b, h, i)),
            pl.BlockSpec((seq, K_DIFF), lambda b, h, i: (b, 0)),
            pl.BlockSpec((None, nk, DV_DIFF, tq), lambda b, h, i: (b, 0, h, 0)),
            pl.BlockSpec((None, 1, LANES), lambda b, h, i: (layer, 0, 0)),
            pl.BlockSpec((None, DV_DIFF, 1), lambda b, h, i: (layer, 0, 0)),
        ],
        out_specs=pl.BlockSpec((tq, LANES), lambda b, h, i: (b * nq + i, h)),
        out_shape=jax.ShapeDtypeStruct((batch * seq, H_DIFF * DV_DIFF), BF16),
        scratch_shapes=[
            pltpu.VMEM((4, K_DIFF, tq), BF16),
            pltpu.VMEM((4, DV_DIFF + 2 * SUBLANES, tq), F32),
            pltpu.VMEM((4, tq, tq), F32),
            pltpu.VMEM((4, tq, tq), F32),
        ],
        compiler_params=_params(("arbitrary", "arbitrary", "arbitrary")),
        name="diff_attn_prompt",
    )(qt, kn, vdb, lam, subg)


def _sb_attn_kernel(qt_ref, k_ref, vt_ref, u_ref, o_ref, qz_sc, acc_sc):
    tq = qt_ref.shape[-1]
    tk = vt_ref.shape[-1]
    nh = qz_sc.shape[0]
    qi = pl.program_id(2)
    @pl.when(qi == 0)
    def _():
        qz_sc[...] = jnp.zeros(qz_sc.shape, BF16)

    for hh in range(nh):
        qz_sc[hh, hh * D_SB:(hh + 1) * D_SB, :] = qt_ref[hh * D_SB:(hh + 1) * D_SB, :]
    acc_sc[...] = jnp.zeros(acc_sc.shape, F32)

    def block(ki, masked, sticks):
        k = k_ref[pl.ds(pl.multiple_of(ki * tk, tk), tk), :]
        vt = vt_ref[ki]
        u = u_ref[...]
        if masked:
            row = lax.broadcasted_iota(jnp.int32, (tk, tq), 0)
            col = lax.broadcasted_iota(jnp.int32, (tk, tq), 1)
            keep = row < col
        zs = [_dot(k, qz_sc[hh]) for hh in range(nh)]
        parts = []
        for hh in range(nh):
            l1 = _log2_one_minus_beta(zs[hh])
            if masked:
                l1 = jnp.where(keep, l1, 0.0)
            l_hi = l1.astype(BF16)
            parts.append(jnp.concatenate([l_hi, (l1 - l_hi.astype(F32)).astype(BF16)], axis=0))
        sufs = [_dot(u, parts[hh]) for hh in range(nh)]
        weights, out = [], []
        for hh in range(nh):
            a = jnp.exp2(zs[hh] + sufs[hh] + sticks[hh])
            if masked:
                a = jnp.where(keep, a, 0.0)
            weights.append(a.astype(BF16))
            out.append(sticks[hh] + sufs[hh][0:1, :])
        for hh in range(nh):
            acc_sc[hh] += _dot(vt[hh * D_SB:(hh + 1) * D_SB, :], weights[hh])
        return tuple(out)

    def more(sticks):
        worst = sticks[0]
        for t in sticks[1:]:
            worst = jnp.maximum(worst, t)
        return jnp.max(worst) > -SB_STOP_LOG2

    sticks = block(qi, True, tuple(jnp.zeros((1, tq), F32) for _ in range(nh)))

    def cond(st):
        return jnp.logical_and(st[0] < qi, st[1])

    def body(st):
        new = block(qi - 1 - st[0], False, st[2])
        return st[0] + 1, more(new), new

    lax.while_loop(cond, body, (jnp.int32(0), more(sticks), sticks))
    o_ref[...] = jnp.concatenate([acc_sc[hh] for hh in range(nh)], axis=0).T.astype(o_ref.dtype)


def _sb_attn(qst, ksn, vsb, tri_t, batch, seq):
    tq = vsb.shape[-1]
    nk = vsb.shape[1]
    nq = seq // tq
    kl = MXU_DIM
    nh = kl // D_SB
    return pl.pallas_call(
        _sb_attn_kernel,
        grid=(batch, H_SB // nh, nq),
        in_specs=[
            pl.BlockSpec((None, kl, tq), lambda b, h, i: (b, h, i)),
            pl.BlockSpec((seq, kl), lambda b, h, i: (b, h)),
            pl.BlockSpec((None, nk, kl, tq), lambda b, h, i: (b, 0, h, 0)),
            pl.BlockSpec((tq, 2 * tq), lambda b, h, i: (0, 0)),
        ],
        out_specs=pl.BlockSpec((tq, kl), lambda b, h, i: (b * nq + i, h)),
        out_shape=jax.ShapeDtypeStruct((batch * seq, H_SB * D_SB), BF16),
        scratch_shapes=[
            pltpu.VMEM((nh, kl, tq), BF16),
            pltpu.VMEM((nh, D_SB, tq), F32),
        ],
        compiler_params=_params(("arbitrary", "arbitrary", "arbitrary")),
        name="sb_attn_prompt",
    )(qst, ksn, vsb, tri_t)


def _ffn_prompt_kernel(x_ref, od_ref, os_ref, mod_ref, g2_ref, wo_ref, wg_ref, wu_ref, wd_ref, fg_ref,
                       o_ref, *, d_model, final):
    d = d_model
    att = _dot(od_ref[...], wo_ref[0:Q_DIFF, :]) + _dot(os_ref[...], wo_ref[Q_DIFF:, :])
    x1 = x_ref[...] + mod_ref[:, 2 * d:3 * d] * att
    h = _norm_mod(x1, g2_ref[...], mod_ref[:, 4 * d:5 * d], mod_ref[:, 3 * d:4 * d]).astype(BF16)
    ff = wg_ref.shape[1]
    acc = None
    for c in range(ff // FF_CHUNK):
        sl = slice(c * FF_CHUNK, (c + 1) * FF_CHUNK)
        a = (_silu(_dot(h, wg_ref[:, sl])) * _dot(h, wu_ref[:, sl])).astype(BF16)
        part = _dot(a, wd_ref[sl, :])
        acc = part if acc is None else acc + part
    y = x1 + mod_ref[:, 5 * d:6 * d] * acc
    if final:
        y = _rmsnorm(y, fg_ref[...])
    o_ref[...] = y


def _ffn_prompt(layer, x2, od, osb, mod_p, g2, wo, wg, wu, wd, fg, seq, final):
    m, d = x2.shape
    tm = min(TOKEN_TILE, seq)
    tps = seq // tm
    ff = wg.shape[-1]
    once = pl.Buffered(1)
    kern = functools.partial(_ffn_prompt_kernel, d_model=d, final=final)
    return pl.pallas_call(
        kern,
        grid=(m // tm,),
        in_specs=[
            pl.BlockSpec((tm, d), lambda i: (i, 0)),
            pl.BlockSpec((tm, Q_DIFF), lambda i: (i, 0)),
            pl.BlockSpec((tm, QKV_SB), lambda i: (i, 0)),
            pl.BlockSpec((None, None, 1, 6 * d), lambda i: (layer, i // tps, 0, 0)),
            pl.BlockSpec((None, 1, d), lambda i: (layer, 0, 0)),
            pl.BlockSpec((None, d, d), lambda i: (layer, 0, 0), pipeline_mode=once),
            pl.BlockSpec((None, d, ff), lambda i: (layer, 0, 0), pipeline_mode=once),
            pl.BlockSpec((None, d, ff), lambda i: (layer, 0, 0), pipeline_mode=once),
            pl.BlockSpec((None, ff, d), lambda i: (layer, 0, 0), pipeline_mode=once),
            pl.BlockSpec((1, d), lambda i: (0, 0)),
        ],
        out_specs=pl.BlockSpec((tm, d), lambda i: (i, 0)),
        out_shape=jax.ShapeDtypeStruct((m, d), F32),
        compiler_params=_params(("arbitrary",)),
        name="ffn_prompt",
    )(x2, od, osb, mod_p, g2, wo, wg, wu, wd, fg)


def _proj_sample_kernel(x_ref, mod_ref, g_ref, wq_ref, wn_ref, wkv_ref, ca_ref, sb_ref, sc_ref, ct_ref, st_ref,
                        qd_ref, qs_ref, kn_ref, vn_ref, kd_ref, vd_ref, ks_ref, vs_ref,
                        *, d_model, q_scale, qs_scale):
    d = d_model
    h = _norm_mod(x_ref[...], g_ref[...], mod_ref[:, d:2 * d], mod_ref[:, 0:d]).astype(BF16)
    ca, sb, sc = ca_ref[...], sb_ref[...], sc_ref[...]
    rq = _dot(h, wq_ref[:, 0:Q_DIFF])
    for j in range(Q_DIFF // LANES):
        qd_ref[:, j * LANES:(j + 1) * LANES] = _rope_lanes(rq[:, j * LANES:(j + 1) * LANES], ca, sb, sc) * q_scale
    qs_ref[...] = _dot(h, wq_ref[:, Q_DIFF:Q_DIFF + QKV_SB]) * qs_scale
    rk = _dot(h, wn_ref[:, 0:K_DIFF])
    for j in range(K_DIFF // LANES):
        kn_ref[:, j * LANES:(j + 1) * LANES] = _rope_lanes(rk[:, j * LANES:(j + 1) * LANES], ca, sb, sc)
    vn_ref[...] = _dot(h, wn_ref[:, K_DIFF:K_DIFF + V_DIFF])
    kd, vd, ks, vs = _split_kv_rows(h, wkv_ref, ct_ref[...], st_ref[...])
    kd_ref[...] = kd
    vd_ref[...] = vd
    ks_ref[...] = ks
    vs_ref[...] = vs


def _proj_sample(layer, xs, mod_all, g1, wq_s, wn, wkv, rope_l, rope_t):
    ns, d = xs.shape
    ca, sb, sc = rope_l
    ct, st = rope_t
    kern = functools.partial(_proj_sample_kernel, d_model=d,
                             q_scale=DK_DIFF ** -0.5 * LOG2E, qs_scale=D_SB ** -0.5 * LOG2E)
    full = lambda shape: pl.BlockSpec(shape, lambda i: (0,) * len(shape))
    out_shape = [
        jax.ShapeDtypeStruct((ns, Q_DIFF), F32),
        jax.ShapeDtypeStruct((ns, QKV_SB), F32),
        jax.ShapeDtypeStruct((ns, K_DIFF), F32),
        jax.ShapeDtypeStruct((ns, V_DIFF), F32),
        jax.ShapeDtypeStruct((K_DIFF, ns), F32),
        jax.ShapeDtypeStruct((V_DIFF, ns), F32),
        jax.ShapeDtypeStruct((QKV_SB, ns), F32),
        jax.ShapeDtypeStruct((QKV_SB, ns), F32),
    ]
    return pl.pallas_call(
        kern,
        grid=(1,),
        in_specs=[
            full((ns, d)),
            pl.BlockSpec((None, ns, 6 * d), lambda i: (layer, 0, 0)),
            pl.BlockSpec((None, 1, d), lambda i: (layer, 0, 0)),
            pl.BlockSpec((None, d, Q_DIFF + QKV_SB), lambda i: (layer, 0, 0)),
            pl.BlockSpec((None, d, K_DIFF + V_DIFF), lambda i: (layer, 0, 0)),
            pl.BlockSpec((None, KV_ROWS, d), lambda i: (layer, 0, 0)),
            full((1, LANES)), full((1, LANES)), full((1, LANES)),
            full((ROT_DIM, 1)), full((ROT_DIM, 1)),
        ],
        out_specs=[full(s.shape) for s in out_shape],
        out_shape=out_shape,
        compiler_params=_params(("arbitrary",)),
        name="proj_sample",
    )(xs, mod_all, g1, wq_s, wn, wkv, ca, sb, sc, ct, st)


def _suffix_sum_lanes(x):
    rows, n = x.shape
    lane = lax.broadcasted_iota(jnp.int32, (rows, LANES), 1)
    tiles = []
    for t in range(n // LANES):
        y = x[:, t * LANES:(t + 1) * LANES]
        sh = 1
        while sh < LANES:
            y = y + jnp.where(lane < LANES - sh, pltpu.roll(y, LANES - sh, 1), 0.0)
            sh *= 2
        tiles.append(y)
    carry = jnp.zeros((rows, 1), F32)
    out = [None] * len(tiles)
    for t in range(len(tiles) - 1, -1, -1):
        out[t] = tiles[t] + carry
        carry = carry + tiles[t][:, 0:1]
    return jnp.concatenate(out, axis=1)


def _decode_attn_kernel(pt_ref, qd_ref, qs_ref, kn_ref, vn_ref, lam_ref, ckd, cvd, cks, cvs,
                        od_ref, os_ref, kd_buf, vd_buf, kt_buf, vt_buf, kr_buf, vr_buf, sem, sem_r,
                        *, layer, n_pages, page):
    i = pl.program_id(0)
    n = pl.num_programs(0)
    nslot = kd_buf.shape[0]
    slot = lax.rem(i, nslot)

    def copies(seq, sl):
        out = []
        for j in range(n_pages):
            pg = pt_ref[seq, j]
            dst = pl.ds(j * page, page)
            out.append(pltpu.make_async_copy(ckd.at[layer, pg], kd_buf.at[sl, :, dst], sem.at[sl, 0]))
            out.append(pltpu.make_async_copy(cvd.at[layer, pg], vd_buf.at[sl, :, dst], sem.at[sl, 1]))
        pg = pt_ref[seq, n_pages - 1]
        out.append(pltpu.make_async_copy(cks.at[layer, pg], kt_buf.at[sl], sem.at[sl, 2]))
        out.append(pltpu.make_async_copy(cvs.at[layer, pg], vt_buf.at[sl], sem.at[sl, 3]))
        return out

    def rest_copies(seq):
        out = []
        for j in range(n_pages - 1):
            pg = pt_ref[seq, j]
            dst = pl.ds(j * page, page)
            out.append(pltpu.make_async_copy(cks.at[layer, pg], kr_buf.at[:, dst], sem_r.at[0]))
            out.append(pltpu.make_async_copy(cvs.at[layer, pg], vr_buf.at[:, dst], sem_r.at[1]))
        return out

    @pl.when(i == 0)
    def _():
        for ahead in range(nslot - 1):
            @pl.when(ahead < n)
            def _():
                for cp in copies(ahead, ahead):
                    cp.start()

    @pl.when(i + nslot - 1 < n)
    def _():
        for cp in copies(i + nslot - 1, lax.rem(i + nslot - 1, nslot)):
            cp.start()

    for cp in copies(i, slot):
        cp.wait()

    rs = lax.broadcasted_iota(jnp.int32, (H_SB, QKV_SB), 0)
    cs = lax.broadcasted_iota(jnp.int32, (H_SB, QKV_SB), 1)
    hsel = (cs >> 6) == rs
    qsm = jnp.where(hsel, jnp.broadcast_to(qs_ref[pl.ds(i, 1), :], (H_SB, QKV_SB)), 0.0)
    z = _dot(qsm, kt_buf[slot])

    r8 = lax.broadcasted_iota(jnp.int32, (SUBLANES, K_DIFF), 0)
    c8 = lax.broadcasted_iota(jnp.int32, (SUBLANES, K_DIFF), 1)
    sel = (c8 >> 5) == r8
    qrow = qd_ref[pl.ds(i, 1), :]
    qm = jnp.concatenate(
        [jnp.where(sel, jnp.broadcast_to(qrow[:, g * K_DIFF:(g + 1) * K_DIFF], (SUBLANES, K_DIFF)), 0.0)
         for g in range(2)], axis=0)
    s = _dot(qm, kd_buf[slot])
    kn = kn_ref[pl.ds(i, 1), :]
    s_self = jnp.sum(qm * kn, axis=1, keepdims=True)
    mx = jnp.maximum(jnp.max(s, axis=1, keepdims=True), s_self)
    p = jnp.exp2(s - mx)
    p_self = jnp.exp2(s_self - mx)
    den = jnp.sum(p, axis=1, keepdims=True) + p_self
    o = _dot_nt(p, vd_buf[slot]) + p_self * vn_ref[pl.ds(i, 1), :]
    r16 = lax.broadcasted_iota(jnp.int32, (2 * SUBLANES, V_DIFF), 0)
    c16 = lax.broadcasted_iota(jnp.int32, (2 * SUBLANES, V_DIFF), 1)
    lam = lam_ref[:, 0:1]
    coef = jnp.where((r16[:, 0:1] & 1) == 1, -lam, 1.0) / den
    w = jnp.where((c16 >> 6) == ((r16 & 7) >> 1), o * coef, 0.0)
    od_ref[pl.ds(i, 1), :] = jnp.concatenate(
        [jnp.sum(w[g * SUBLANES:(g + 1) * SUBLANES], axis=0, keepdims=True) for g in range(2)], axis=1)

    suf = _suffix_sum_lanes(_log2_one_minus_beta(z))
    o_new = _dot_nt(jnp.exp2(z + suf), vt_buf[slot])
    stick = suf[:, 0:1]
    os_ref[pl.ds(i, 1), :] = jnp.sum(jnp.where(hsel, o_new, 0.0), axis=0, keepdims=True)

    @pl.when(jnp.max(stick) > -SB_STOP_LOG2)
    def _():
        for cp in rest_copies(i):
            cp.start()
        for cp in rest_copies(i):
            cp.wait()
        z_r = _dot(qsm, kr_buf[...])
        suf_r = _suffix_sum_lanes(_log2_one_minus_beta(z_r))
        o_all = o_new + _dot_nt(jnp.exp2(z_r + suf_r + stick), vr_buf[...])
        os_ref[pl.ds(i, 1), :] = jnp.sum(jnp.where(hsel, o_all, 0.0), axis=0, keepdims=True)


def _decode_attn(layer, page_table, qd, qs, kn, vn, lam, ckd, cvd, cks, cvs):
    ns, n_pages = page_table.shape
    page = ckd.shape[-1]
    past = n_pages * page
    kern = functools.partial(_decode_attn_kernel, layer=layer, n_pages=n_pages, page=page)
    res = lambda shape: pl.BlockSpec(shape, lambda i, pt: (0,) * len(shape))
    any_spec = pl.BlockSpec(memory_space=pl.ANY)
    grid_spec = pltpu.PrefetchScalarGridSpec(
        num_scalar_prefetch=1,
        grid=(ns,),
        in_specs=[
            res((ns, Q_DIFF)), res((ns, QKV_SB)), res((ns, K_DIFF)), res((ns, V_DIFF)),
            pl.BlockSpec((None, 1, LANES), lambda i, pt: (layer, 0, 0)),
            any_spec, any_spec, any_spec, any_spec,
        ],
        out_specs=[res((ns, H_DIFF * DV_DIFF)), res((ns, QKV_SB))],
        scratch_shapes=[
            pltpu.VMEM((DECODE_SLOTS, K_DIFF, past), F32),
            pltpu.VMEM((DECODE_SLOTS, V_DIFF, past), F32),
            pltpu.VMEM((DECODE_SLOTS, QKV_SB, page), F32),
            pltpu.VMEM((DECODE_SLOTS, QKV_SB, page), F32),
            pltpu.VMEM((QKV_SB, past - page), F32),
            pltpu.VMEM((QKV_SB, past - page), F32),
            pltpu.SemaphoreType.DMA((DECODE_SLOTS, 4)),
            pltpu.SemaphoreType.DMA((2,)),
        ],
    )
    return pl.pallas_call(
        kern,
        grid_spec=grid_spec,
        out_shape=[jax.ShapeDtypeStruct((ns, H_DIFF * DV_DIFF), F32), jax.ShapeDtypeStruct((ns, QKV_SB), F32)],
        compiler_params=_params(("arbitrary",)),
        name="decode_attn",
    )(page_table, qd, qs, kn, vn, lam, ckd, cvd, cks, cvs)


def _ffn_sample_kernel(x_ref, od_ref, os_ref, mod_ref, g2_ref, gs_ref, seg_ref, wo_ref, wg_ref, wu_ref, wd_ref,
                       fg_ref, o_ref, x1_sc, h_sc, acc_sc, *, d_model, out_scale, final):
    d = d_model
    c = pl.program_id(0)

    @pl.when(c == 0)
    def _():
        o = od_ref[...]
        ss = o * o
        s_hi = ss.astype(BF16)
        s_lo = (ss - s_hi.astype(F32)).astype(BF16)
        seg = _dot(s_hi, seg_ref[...]) + _dot(s_lo, seg_ref[...])
        y = (o * lax.rsqrt(seg * (1.0 / DV_DIFF) + EPS)) * gs_ref[...] * out_scale
        att = _dot(y.astype(BF16), wo_ref[0:Q_DIFF, :]) + _dot(os_ref[...].astype(BF16), wo_ref[Q_DIFF:, :])
        x1 = x_ref[...] + mod_ref[:, 2 * d:3 * d] * att
        x1_sc[...] = x1
        h_sc[...] = _norm_mod(x1, g2_ref[...], mod_ref[:, 4 * d:5 * d], mod_ref[:, 3 * d:4 * d]).astype(BF16)
        acc_sc[...] = jnp.zeros(acc_sc.shape, F32)

    h = h_sc[...]
    a = (_silu(_dot(h, wg_ref[...])) * _dot(h, wu_ref[...])).astype(BF16)
    acc_sc[...] += _dot(a, wd_ref[...])

    @pl.when(c == pl.num_programs(0) - 1)
    def _():
        y = x1_sc[...] + mod_ref[:, 5 * d:6 * d] * acc_sc[...]
        if final:
            y = _rmsnorm(y, fg_ref[...])
        o_ref[...] = y


def _ffn_sample(layer, xs, od, osb, mod_all, g2, subg8, seg, wo_s, wg, wu, wd, fg, out_scale, final):
    ns, d = xs.shape
    ff = wg.shape[-1]
    kern = functools.partial(_ffn_sample_kernel, d_model=d, out_scale=out_scale, final=final)
    full = lambda shape: pl.BlockSpec(shape, lambda c: (0,) * len(shape))
    return pl.pallas_call(
        kern,
        grid=(ff // FF_CHUNK,),
        in_specs=[
            full((ns, d)), full((ns, Q_DIFF)), full((ns, QKV_SB)),
            pl.BlockSpec((None, ns, 6 * d), lambda c: (layer, 0, 0)),
            pl.BlockSpec((None, 1, d), lambda c: (layer, 0, 0)),
            pl.BlockSpec((None, 1, Q_DIFF), lambda c: (layer, 0, 0)),
            full((Q_DIFF, Q_DIFF)),
            pl.BlockSpec((None, d, d), lambda c: (layer, 0, 0)),
            pl.BlockSpec((None, d, FF_CHUNK), lambda c: (layer, 0, c)),
            pl.BlockSpec((None, d, FF_CHUNK), lambda c: (layer, 0, c)),
            pl.BlockSpec((None, FF_CHUNK, d), lambda c: (layer, c, 0)),
            full((1, d)),
        ],
        out_specs=full((ns, d)),
        out_shape=jax.ShapeDtypeStruct((ns, d), F32),
        scratch_shapes=[pltpu.VMEM((ns, d), F32), pltpu.VMEM((ns, d), BF16), pltpu.VMEM((ns, d), F32)],
        compiler_params=_params(("arbitrary",)),
        name="ffn_sample",
    )(xs, od, osb, mod_all, g2, subg8, seg, wo_s, wg, wu, wd, fg)


def _rope_tables(pos):
    half = ROT_DIM // 2
    inv = ROPE_THETA ** (-jnp.arange(0, ROT_DIM, 2, dtype=F32) / ROT_DIM)
    ang = pos.astype(F32)[:, None] * inv[None, :]
    cos, sin = jnp.cos(ang), jnp.sin(ang)
    dl = np.arange(LANES) % DK_DIFF
    jl = dl % half
    ca = jnp.where(dl < ROT_DIM, cos[:, jl], 1.0)
    sb = jnp.where(dl < half, -sin[:, jl], 0.0)
    sc = jnp.where((dl >= half) & (dl < ROT_DIM), sin[:, jl], 0.0)
    dr = np.arange(ROT_DIM)
    ct = cos[:, dr % half].T
    st = jnp.where((dr < half)[:, None], -sin[:, dr % half].T, sin[:, dr % half].T)
    return (ca, sb, sc), (ct, st)


def kernel(x_prompt, x_sample, c_prompt, c_sample, cache_diff_k, cache_diff_v, cache_sb_k, cache_sb_v,
           page_table, norm1_g, norm2_g, w_ada, b_ada, w_in, w_o, lam_q1, lam_k1, lam_q2, lam_k2,
           subln_g, w_gate, w_up, w_down, final_g):
    batch, seq, d = x_prompt.shape
    ns = x_sample.shape[0]
    depth = w_in.shape[0]
    n_pool, page = cache_diff_k.shape[1], cache_diff_k.shape[2]
    n_pages = page_table.shape[1]
    past = n_pages * page
    assert x_sample.shape[1] == 1 and seq % min(ATT_BLOCK, seq) == 0

    c0, c1, c2, c3, c4 = Q_DIFF, Q_DIFF + K_DIFF, Q_DIFF + K_DIFF + V_DIFF, \
        Q_DIFF + K_DIFF + V_DIFF + QKV_SB, Q_DIFF + K_DIFF + V_DIFF + 2 * QKV_SB
    w_in_b = w_in.astype(BF16)
    perm = np.array([128 * h + 64 * g + e for g in range(2) for h in range(KV_DIFF) for e in range(64)])
    wq_s = jnp.concatenate([w_in_b[:, :, :c0][:, :, perm], w_in_b[:, :, c2:c3]], axis=2)
    wn = w_in_b[:, :, c0:c2]
    wkn = jnp.concatenate([w_in_b[:, :, c0:c1], w_in_b[:, :, c3:c4]], axis=2)
    w_in_t = jnp.swapaxes(w_in_b, 1, 2)
    wqt = jnp.concatenate([w_in_t[:, :c0], w_in_t[:, c2:c3]], axis=1)
    wkv = jnp.concatenate([w_in_t[:, c0:c2], w_in_t[:, c3:]], axis=1)
    wo = w_o.astype(BF16)
    wo_s = jnp.concatenate([wo[:, :Q_DIFF][:, perm], wo[:, Q_DIFF:]], axis=1)
    wg, wu, wd = w_gate.astype(BF16), w_up.astype(BF16), w_down.astype(BF16)

    g1 = norm1_g.reshape(depth, 1, d)
    g2 = norm2_g.reshape(depth, 1, d)
    fg = final_g.reshape(1, d)
    subg = subln_g.reshape(depth, DV_DIFF, 1)
    subg8 = jnp.tile(subln_g, (1, H_DIFF)).reshape(depth, 1, H_DIFF * DV_DIFF)
    lam_inits = [0.8 - 0.6 * math.exp(-0.3 * l) for l in range(depth)]
    lam = _lambdas(lam_q1, lam_k1, lam_q2, lam_k2, jnp.asarray(lam_inits, F32).reshape(depth, 1))
    lam = lam.reshape(depth, 1, LANES)

    mod_all = _modulation(jnp.concatenate([c_sample, c_prompt], axis=0), w_ada, b_ada)
    mod_p = mod_all[:, ns:].reshape(depth, batch, 1, 6 * d)

    rope_pl, rope_pt = _rope_tables(jnp.arange(seq, dtype=jnp.int32))
    rope_sl, rope_st = _rope_tables(past + jnp.arange(1, dtype=jnp.int32))

    tk = min(ATT_BLOCK, seq)
    tri_t = (np.arange(tk)[None, :] >= np.arange(tk)[:, None]).astype(np.float32)
    tri_t = jnp.asarray(np.concatenate([tri_t, tri_t], axis=1), BF16)
    segm = (np.arange(Q_DIFF)[:, None] // DV_DIFF == np.arange(Q_DIFF)[None, :] // DV_DIFF).astype(np.float32)
    segm = jnp.asarray(segm, BF16)

    def cache_rows(c):
        return jnp.transpose(c, (0, 1, 3, 4, 2)).reshape(depth, n_pool, c.shape[3] * c.shape[4], page)

    ckd, cvd, cks, cvs = (cache_rows(c) for c in (cache_diff_k, cache_diff_v, cache_sb_k, cache_sb_v))

    xp = x_prompt.reshape(batch * seq, d)
    xs = x_sample.reshape(ns, d)
    rows_p, rows_s = [], []
    for l in range(depth):
        out_scale = 1.0 - lam_inits[l]
        final = l == depth - 1
        (qt, qst, kn, ksn, kd, vd, ks, vs, vdb, vsb) = _proj_prompt(
            l, xp, mod_p, g1, wqt, wkv, wkn, rope_pl, rope_pt, batch, seq)
        od = _diff_attn(l, qt, kn, vdb, lam, subg, batch, seq, out_scale)
        osb = _sb_attn(qst, ksn, vsb, tri_t, batch, seq)
        xp = _ffn_prompt(l, xp, od, osb, mod_p, g2, wo, wg, wu, wd, fg, seq, final)
        rows_p.append((kd, vd, ks, vs))

        (qd_s, qs_s, kn_s, vn_s, kd_s, vd_s, ks_s, vs_s) = _proj_sample(
            l, xs, mod_all, g1, wq_s, wn, wkv, rope_sl, rope_st)
        od_s, os_s = _decode_attn(l, page_table, qd_s, qs_s, kn_s, vn_s, lam, ckd, cvd, cks, cvs)
        xs = _ffn_sample(l, xs, od_s, os_s, mod_all, g2, subg8, segm, wo_s, wg, wu, wd, fg, out_scale, final)
        rows_s.append((kd_s, vd_s, ks_s, vs_s))

    def prompt_rows(idx, heads):
        a = jnp.stack([r[idx] for r in rows_p])
        return jnp.transpose(a.reshape(depth, batch, heads, HEAD_DIM, seq), (0, 1, 4, 2, 3))

    def sample_rows(idx, heads):
        a = jnp.stack([r[idx] for r in rows_s])
        return jnp.transpose(a.reshape(depth, heads, HEAD_DIM, ns), (0, 3, 1, 2))[:, :, None]

    return (xp.reshape(batch, seq, d), xs.reshape(ns, 1, d),
            prompt_rows(0, KV_DIFF), prompt_rows(1, KV_DIFF), prompt_rows(2, H_SB), prompt_rows(3, H_SB),
            sample_rows(0, KV_DIFF), sample_rows(1, KV_DIFF), sample_rows(2, H_SB), sample_rows(3, H_SB))
```

```python
import functools
import math

import jax
import jax.numpy as jnp
import numpy as np
from jax import lax
from jax.experimental import pallas as pl
from jax.experimental.pallas import tpu as pltpu

F32 = jnp.float32
BF16 = jnp.bfloat16

HEAD_DIM = 64
H_DIFF = 8
KV_DIFF = 4
DK_DIFF = HEAD_DIM // 2
DV_DIFF = HEAD_DIM
H_SB = 8
D_SB = HEAD_DIM
ROT_DIM = DK_DIFF // 4
ROPE_THETA = 500000.0
EPS = 1e-6
NEG = -1e30
LOG2E = 1.4426950408889634
Q_DIFF = H_DIFF * 2 * DK_DIFF
K_DIFF = KV_DIFF * 2 * DK_DIFF
V_DIFF = KV_DIFF * DV_DIFF
QKV_SB = H_SB * D_SB
KV_ROWS = K_DIFF + V_DIFF + 2 * QKV_SB

LANES = 128
SUBLANES = 8
MXU_DIM = 256
VMEM_LIMIT = 56 * 1024 * 1024

ATT_BLOCK = 256
TOKEN_TILE = 512
FF_CHUNK = 256
DECODE_SLOTS = 3
SB_STOP_LOG2 = 100.0


def _params(sem, vmem=VMEM_LIMIT):
    return pltpu.CompilerParams(dimension_semantics=sem, vmem_limit_bytes=vmem)


def _dot(a, b):
    return jnp.dot(a, b, preferred_element_type=F32)


def _dot_nt(a, b):
    return lax.dot_general(a, b, (((1,), (1,)), ((), ())), preferred_element_type=F32)


def _silu(x):
    return x * (1.0 / (1.0 + jnp.exp(-x)))


def _norm_mod(x, g, scale, shift):
    ms = jnp.mean(x * x, axis=-1, keepdims=True)
    return (x * lax.rsqrt(ms + EPS)) * g * (1.0 + scale) + shift


def _rmsnorm(x, g):
    ms = jnp.mean(x * x, axis=-1, keepdims=True)
    return (x * lax.rsqrt(ms + EPS)) * g


def _log2_one_minus_beta(z):
    return jnp.log(1.0 + jnp.exp2(-jnp.abs(z))) * (-LOG2E) - jnp.maximum(z, 0.0)


def _rope_lanes(t, ca, sb, sc):
    return t * ca + pltpu.roll(t, LANES - ROT_DIM // 2, 1) * sb + pltpu.roll(t, ROT_DIM // 2, 1) * sc


def _rope_rows(slab, cos_t, sin_t):
    return slab * cos_t + pltpu.roll(slab, ROT_DIM // 2, 0) * sin_t


def _mod_kernel(c_ref, w_ref, b_ref, o_ref):
    s = _silu(c_ref[...]).astype(BF16)
    o_ref[...] = _dot(s, w_ref[...].astype(BF16)) + b_ref[...]


def _modulation(c_all, w_ada, b_ada):
    depth, d, n = w_ada.shape
    rows = c_all.shape[0]
    tn = 1536
    return pl.pallas_call(
        _mod_kernel,
        grid=(depth, n // tn),
        in_specs=[
            pl.BlockSpec((rows, d), lambda l, j: (0, 0)),
            pl.BlockSpec((None, d, tn), lambda l, j: (l, 0, j)),
            pl.BlockSpec((None, 1, tn), lambda l, j: (l, 0, j)),
        ],
        out_specs=pl.BlockSpec((None, rows, tn), lambda l, j: (l, 0, j)),
        out_shape=jax.ShapeDtypeStruct((depth, rows, n), F32),
        compiler_params=_params(("arbitrary", "arbitrary")),
        name="adaln_mod",
    )(c_all, w_ada, b_ada.reshape(depth, 1, n))


def _lam_kernel(q1_ref, k1_ref, q2_ref, k2_ref, li_ref, o_ref):
    a = jnp.sum(q1_ref[...] * k1_ref[...], axis=-1, keepdims=True)
    b = jnp.sum(q2_ref[...] * k2_ref[...], axis=-1, keepdims=True)
    lam = jnp.exp(a) - jnp.exp(b) + li_ref[...]
    o_ref[...] = jnp.broadcast_to(lam, o_ref.shape)


def _lambdas(lam_q1, lam_k1, lam_q2, lam_k2, lam_init):
    depth = lam_q1.shape[0]
    return pl.pallas_call(
        _lam_kernel,
        out_shape=jax.ShapeDtypeStruct((depth, LANES), F32),
        name="diff_lambda",
    )(lam_q1, lam_k1, lam_q2, lam_k2, lam_init)


def _rope_all_rows(x, cos_t, sin_t):
    parts = []
    for g in range(x.shape[0] // DK_DIFF):
        r0 = g * DK_DIFF
        parts.append(_rope_rows(x[r0:r0 + ROT_DIM, :], cos_t, sin_t))
        parts.append(x[r0 + ROT_DIM:r0 + DK_DIFF, :])
    return jnp.concatenate(parts, axis=0)


def _split_kv_rows(h, wkv_ref, cos_t, sin_t):
    kd = _rope_all_rows(_dot_nt(wkv_ref[0:K_DIFF, :], h), cos_t, sin_t)
    vd = _dot_nt(wkv_ref[K_DIFF:K_DIFF + V_DIFF, :], h)
    ks = _dot_nt(wkv_ref[K_DIFF + V_DIFF:K_DIFF + V_DIFF + QKV_SB, :], h)
    vs = _dot_nt(wkv_ref[K_DIFF + V_DIFF + QKV_SB:KV_ROWS, :], h)
    return kd, vd, ks, vs


def _proj_prompt_kernel(x_ref, mod_ref, g_ref, wqt_ref, wkv_ref, wkn_ref, ca_ref, sb_ref, sc_ref, ct_ref, st_ref,
                        kd_all, vd_all, ks_all, vs_all,
                        qt_ref, qst_ref, kn_ref, ksn_ref, kd_ref, vd_ref, ks_ref, vs_ref, vdb_ref, vsb_ref,
                        *, d_model, q_scale, qs_scale):
    del kd_all, vd_all, ks_all, vs_all
    d = d_model
    h = _norm_mod(x_ref[...], g_ref[...], mod_ref[:, d:2 * d], mod_ref[:, 0:d]).astype(BF16)
    ct, st = ct_ref[...], st_ref[...]
    qt = _rope_all_rows(_dot_nt(wqt_ref[0:Q_DIFF, :], h), ct, st)
    qt_ref[...] = (qt * q_scale).astype(BF16)
    qst_ref[...] = (_dot_nt(wqt_ref[Q_DIFF:Q_DIFF + QKV_SB, :], h) * qs_scale).astype(BF16)
    ca, sb, sc = ca_ref[...], sb_ref[...], sc_ref[...]
    rk = _dot(h, wkn_ref[:, 0:K_DIFF])
    for j in range(K_DIFF // LANES):
        sl = slice(j * LANES, (j + 1) * LANES)
        kn_ref[:, sl] = _rope_lanes(rk[:, sl], ca, sb, sc).astype(BF16)
    ksn_ref[...] = _dot(h, wkn_ref[:, K_DIFF:K_DIFF + QKV_SB]).astype(BF16)
    kd, vd, ks, vs = _split_kv_rows(h, wkv_ref, ct, st)
    kd_ref[...] = kd
    vd_ref[...] = vd
    ks_ref[...] = ks
    vs_ref[...] = vs
    nblk = vdb_ref.shape[0]
    tk = vdb_ref.shape[-1]
    for j in range(nblk):
        sl = slice(j * tk, (j + 1) * tk)
        vdb_ref[j] = vd[:, sl].astype(BF16)
        vsb_ref[j] = vs[:, sl].astype(BF16)


def _proj_prompt(layer, x2, mod_p, g1, wqt, wkv, wkn, rope_l, rope_t, stacked, batch, seq):
    m, d = x2.shape
    depth = stacked[0].shape[0]
    tm = min(TOKEN_TILE, seq)
    tk = min(ATT_BLOCK, seq)
    tps = seq // tm
    nblk = tm // tk
    nk = seq // tk
    ca, sb, sc = rope_l
    ct, st = rope_t
    kern = functools.partial(_proj_prompt_kernel, d_model=d,
                             q_scale=DK_DIFF ** -0.5 * LOG2E, qs_scale=D_SB ** -0.5 * LOG2E)
    tok = lambda i: (i // tps, 0, i % tps)
    lay = lambda i: (layer, i // tps, 0, i % tps)
    blk = lambda i: (i // tps, i % tps, 0, 0)
    out_shape = [
        jax.ShapeDtypeStruct((batch, Q_DIFF, seq), BF16),
        jax.ShapeDtypeStruct((batch, QKV_SB, seq), BF16),
        jax.ShapeDtypeStruct((m, K_DIFF), BF16),
        jax.ShapeDtypeStruct((m, QKV_SB), BF16),
        jax.ShapeDtypeStruct((depth, batch, K_DIFF, seq), F32),
        jax.ShapeDtypeStruct((depth, batch, V_DIFF, seq), F32),
        jax.ShapeDtypeStruct((depth, batch, QKV_SB, seq), F32),
        jax.ShapeDtypeStruct((depth, batch, QKV_SB, seq), F32),
        jax.ShapeDtypeStruct((batch, nk, V_DIFF, tk), BF16),
        jax.ShapeDtypeStruct((batch, nk, QKV_SB, tk), BF16),
    ]
    out_specs = [
        pl.BlockSpec((None, Q_DIFF, tm), tok),
        pl.BlockSpec((None, QKV_SB, tm), tok),
        pl.BlockSpec((tm, K_DIFF), lambda i: (i, 0)),
        pl.BlockSpec((tm, QKV_SB), lambda i: (i, 0)),
        pl.BlockSpec((None, None, K_DIFF, tm), lay),
        pl.BlockSpec((None, None, V_DIFF, tm), lay),
        pl.BlockSpec((None, None, QKV_SB, tm), lay),
        pl.BlockSpec((None, None, QKV_SB, tm), lay),
        pl.BlockSpec((None, nblk, V_DIFF, tk), blk),
        pl.BlockSpec((None, nblk, QKV_SB, tk), blk),
    ]
    in_specs = [
        pl.BlockSpec((tm, d), lambda i: (i, 0)),
        pl.BlockSpec((None, None, 1, 6 * d), lambda i: (layer, i // tps, 0, 0)),
        pl.BlockSpec((None, 1, d), lambda i: (layer, 0, 0)),
        pl.BlockSpec((None, Q_DIFF + QKV_SB, d), lambda i: (layer, 0, 0)),
        pl.BlockSpec((None, KV_ROWS, d), lambda i: (layer, 0, 0)),
        pl.BlockSpec((None, d, K_DIFF + QKV_SB), lambda i: (layer, 0, 0)),
        pl.BlockSpec((tm, LANES), lambda i: (i % tps, 0)),
        pl.BlockSpec((tm, LANES), lambda i: (i % tps, 0)),
        pl.BlockSpec((tm, LANES), lambda i: (i % tps, 0)),
        pl.BlockSpec((ROT_DIM, tm), lambda i: (0, i % tps)),
        pl.BlockSpec((ROT_DIM, tm), lambda i: (0, i % tps)),
    ] + [pl.BlockSpec(memory_space=pl.ANY)] * 4
    n_in = len(in_specs)
    return pl.pallas_call(
        kern, grid=(m // tm,), in_specs=in_specs, out_specs=out_specs, out_shape=out_shape,
        input_output_aliases={n_in - 4 + j: 4 + j for j in range(4)},
        compiler_params=_params(("arbitrary",)), name="proj_prompt",
    )(x2, mod_p, g1, wqt, wkv, wkn, ca, sb, sc, ct, st, *stacked)


def _diff_attn_kernel(qt_ref, k_ref, vt_ref, lam_ref, g_ref, o_ref, qz_sc, acc_sc, sa_sc, sb_sc, *, out_scale):
    tq = qt_ref.shape[-1]
    tk = vt_ref.shape[-1]
    h = pl.program_id(1)
    qi = pl.program_id(2)
    @pl.when(qi == 0)
    def _():
        qz_sc[...] = jnp.zeros(qz_sc.shape, BF16)

    for g in range(2):
        for mp in range(2):
            src = g * 2 * DK_DIFF + mp * DK_DIFF
            dst = pl.multiple_of(h * 2 * DK_DIFF + mp * DK_DIFF, DK_DIFF)
            qz_sc[2 * g + mp, pl.ds(dst, DK_DIFF), :] = qt_ref[src:src + DK_DIFF, :]
    acc_sc[...] = jnp.zeros(acc_sc.shape, F32)
    ones = jnp.ones((acc_sc.shape[1] - DV_DIFF, tk), BF16)

    def scores_to(dst, ki):
        k = k_ref[pl.ds(pl.multiple_of(ki * tk, tk), tk), :]
        for gm in range(4):
            dst[gm] = _dot(k, qz_sc[gm])

    def block(ki, masked, src, ms):
        vt = jnp.concatenate([vt_ref[ki], ones], axis=0)
        if masked:
            row = lax.broadcasted_iota(jnp.int32, (tk, tq), 0)
            col = lax.broadcasted_iota(jnp.int32, (tk, tq), 1)
            keep = row <= col
        new_ms, ps, alphas = [], [], []
        for gm in range(4):
            s = src[gm]
            if masked:
                s = jnp.where(keep, s, NEG)
            m_new = jnp.maximum(ms[gm], jnp.max(s, axis=0, keepdims=True))
            alphas.append(jnp.exp2(ms[gm] - m_new))
            ps.append(jnp.exp2(s - m_new).astype(BF16))
            new_ms.append(m_new)
        for gm in range(4):
            acc_sc[gm] = acc_sc[gm] * alphas[gm] + _dot(vt, ps[gm])
        return tuple(new_ms)

    scores_to(sa_sc, 0)

    def body(j, ms):
        scores_to(sb_sc, 2 * j + 1)
        ms = block(2 * j, False, sa_sc, ms)
        scores_to(sa_sc, 2 * j + 2)
        return block(2 * j + 1, False, sb_sc, ms)

    ms = lax.fori_loop(0, qi // 2, body, tuple(jnp.full((1, tq), NEG, F32) for _ in range(4)))

    @pl.when(qi % 2 == 1)
    def _():
        scores_to(sb_sc, qi)
        block(qi, True, sb_sc, block(qi - 1, False, sa_sc, ms))

    @pl.when(qi % 2 == 0)
    def _():
        block(qi, True, sa_sc, ms)

    lam = lam_ref[:, 0:1]
    outs = []
    for g in range(2):
        a0, a1 = acc_sc[2 * g], acc_sc[2 * g + 1]
        o = (a0[0:DV_DIFF] / a0[DV_DIFF:DV_DIFF + 1] - lam * (a1[0:DV_DIFF] / a1[DV_DIFF:DV_DIFF + 1]))
        ms2 = jnp.mean(o * o, axis=0, keepdims=True)
        outs.append((o * lax.rsqrt(ms2 + EPS)) * g_ref[...] * out_scale)
    o_ref[...] = jnp.concatenate(outs, axis=0).T.astype(o_ref.dtype)


def _diff_attn(layer, qt, kn, vdb, lam, subg, batch, seq, out_scale):
    tq = vdb.shape[-1]
    nk = vdb.shape[1]
    nq = seq // tq
    kern = functools.partial(_diff_attn_kernel, out_scale=out_scale)
    return pl.pallas_call(
        kern,
        grid=(batch, KV_DIFF, nq),
        in_specs=[
            pl.BlockSpec((None, 2 * 2 * DK_DIFF, tq), lambda b, h, i: (b, h, i)),
            pl.BlockSpec((seq, K_DIFF), lambda b, h, i: (b, 0)),
            pl.BlockSpec((None, nk, DV_DIFF, tq), lambda b, h, i: (b, 0, h, 0)),
            pl.BlockSpec((None, 1, LANES), lambda b, h, i: (layer, 0, 0)),
            pl.BlockSpec((None, DV_DIFF, 1), lambda b, h, i: (layer, 0, 0)),
        ],
        out_specs=pl.BlockSpec((tq, LANES), lambda b, h, i: (b * nq + i, h)),
        out_shape=jax.ShapeDtypeStruct((batch * seq, H_DIFF * DV_DIFF), BF16),
        scratch_shapes=[
            pltpu.VMEM((4, K_DIFF, tq), BF16),
            pltpu.VMEM((4, DV_DIFF + 2 * SUBLANES, tq), F32),
            pltpu.VMEM((4, tq, tq), F32),
            pltpu.VMEM((4, tq, tq), F32),
        ],
        compiler_params=_params(("arbitrary", "arbitrary", "arbitrary")),
        name="diff_attn_prompt",
    )(qt, kn, vdb, lam, subg)


def _sb_attn_kernel(qt_ref, k_ref, vt_ref, u_ref, o_ref, qz_sc, acc_sc):
    tq = qt_ref.shape[-1]
    tk = vt_ref.shape[-1]
    nh = qz_sc.shape[0]
    kl = qz_sc.shape[1]
    per = kl // D_SB
    qi = pl.program_id(2)
    @pl.when(qi == 0)
    def _():
        qz_sc[...] = jnp.zeros(qz_sc.shape, BF16)

    for hh in range(nh):
        r0 = (hh % per) * D_SB
        qz_sc[hh, r0:r0 + D_SB, :] = qt_ref[hh * D_SB:(hh + 1) * D_SB, :]
    acc_sc[...] = jnp.zeros(acc_sc.shape, F32)

    def block(ki, masked, sticks):
        k = k_ref[pl.ds(pl.multiple_of(ki * tk, tk), tk), :]
        vt = vt_ref[ki]
        u = u_ref[...]
        if masked:
            row = lax.broadcasted_iota(jnp.int32, (tk, tq), 0)
            col = lax.broadcasted_iota(jnp.int32, (tk, tq), 1)
            keep = row < col
        zs = [_dot(k[:, (hh // per) * kl:(hh // per + 1) * kl], qz_sc[hh])
              for hh in range(nh)]
        parts = []
        for hh in range(nh):
            l1 = _log2_one_minus_beta(zs[hh])
            if masked:
                l1 = jnp.where(keep, l1, 0.0)
            l_hi = l1.astype(BF16)
            parts.append(jnp.concatenate([l_hi, (l1 - l_hi.astype(F32)).astype(BF16)], axis=0))
        sufs = [_dot(u, parts[hh]) for hh in range(nh)]
        weights, out = [], []
        for hh in range(nh):
            a = jnp.exp2(zs[hh] + sufs[hh] + sticks[hh])
            if masked:
                a = jnp.where(keep, a, 0.0)
            weights.append(a.astype(BF16))
            out.append(sticks[hh] + sufs[hh][0:1, :])
        for hh in range(nh):
            acc_sc[hh] += _dot(vt[hh * D_SB:(hh + 1) * D_SB, :], weights[hh])
        return tuple(out)

    def more(sticks):
        worst = sticks[0]
        for t in sticks[1:]:
            worst = jnp.maximum(worst, t)
        return jnp.max(worst) > -SB_STOP_LOG2

    sticks = block(qi, True, tuple(jnp.zeros((1, tq), F32) for _ in range(nh)))

    def cond(st):
        return jnp.logical_and(st[0] < qi, st[1])

    def body(st):
        new = block(qi - 1 - st[0], False, st[2])
        return st[0] + 1, more(new), new

    lax.while_loop(cond, body, (jnp.int32(0), more(sticks), sticks))
    o_ref[...] = jnp.concatenate([acc_sc[hh] for hh in range(nh)], axis=0).T.astype(o_ref.dtype)


def _sb_attn(qst, ksn, vsb, tri_t, batch, seq):
    tq = vsb.shape[-1]
    nk = vsb.shape[1]
    nq = seq // tq
    kl = MXU_DIM
    nh = H_SB
    hl = nh * D_SB
    return pl.pallas_call(
        _sb_attn_kernel,
        grid=(batch, H_SB // nh, nq),
        in_specs=[
            pl.BlockSpec((None, hl, tq), lambda b, h, i: (b, h, i)),
            pl.BlockSpec((seq, hl), lambda b, h, i: (b, h)),
            pl.BlockSpec((None, nk, hl, tq), lambda b, h, i: (b, 0, h, 0)),
            pl.BlockSpec((tq, 2 * tq), lambda b, h, i: (0, 0)),
        ],
        out_specs=pl.BlockSpec((tq, hl), lambda b, h, i: (b * nq + i, h)),
        out_shape=jax.ShapeDtypeStruct((batch * seq, H_SB * D_SB), BF16),
        scratch_shapes=[
            pltpu.VMEM((nh, kl, tq), BF16),
            pltpu.VMEM((nh, D_SB, tq), F32),
        ],
        compiler_params=_params(("arbitrary", "arbitrary", "arbitrary")),
        name="sb_attn_prompt",
    )(qst, ksn, vsb, tri_t)


def _ffn_prompt_kernel(x_ref, od_ref, os_ref, mod_ref, g2_ref, wo_ref, wg_ref, wu_ref, wd_ref, fg_ref,
                       o_ref, *, d_model, final):
    d = d_model
    att = _dot(od_ref[...], wo_ref[0:Q_DIFF, :]) + _dot(os_ref[...], wo_ref[Q_DIFF:, :])
    x1 = x_ref[...] + mod_ref[:, 2 * d:3 * d] * att
    h = _norm_mod(x1, g2_ref[...], mod_ref[:, 4 * d:5 * d], mod_ref[:, 3 * d:4 * d]).astype(BF16)
    ff = wg_ref.shape[1]
    acc = None
    for c in range(ff // FF_CHUNK):
        sl = slice(c * FF_CHUNK, (c + 1) * FF_CHUNK)
        a = (_silu(_dot(h, wg_ref[:, sl])) * _dot(h, wu_ref[:, sl])).astype(BF16)
        part = _dot(a, wd_ref[sl, :])
        acc = part if acc is None else acc + part
    y = x1 + mod_ref[:, 5 * d:6 * d] * acc
    if final:
        y = _rmsnorm(y, fg_ref[...])
    o_ref[...] = y


def _ffn_prompt(layer, x2, od, osb, mod_p, g2, wo, wg, wu, wd, fg, seq, final):
    m, d = x2.shape
    tm = min(TOKEN_TILE, seq)
    tps = seq // tm
    ff = wg.shape[-1]
    once = pl.Buffered(1)
    kern = functools.partial(_ffn_prompt_kernel, d_model=d, final=final)
    return pl.pallas_call(
        kern,
        grid=(m // tm,),
        in_specs=[
            pl.BlockSpec((tm, d), lambda i: (i, 0)),
            pl.BlockSpec((tm, Q_DIFF), lambda i: (i, 0)),
            pl.BlockSpec((tm, QKV_SB), lambda i: (i, 0)),
            pl.BlockSpec((None, None, 1, 6 * d), lambda i: (layer, i // tps, 0, 0)),
            pl.BlockSpec((None, 1, d), lambda i: (layer, 0, 0)),
            pl.BlockSpec((None, d, d), lambda i: (layer, 0, 0), pipeline_mode=once),
            pl.BlockSpec((None, d, ff), lambda i: (layer, 0, 0), pipeline_mode=once),
            pl.BlockSpec((None, d, ff), lambda i: (layer, 0, 0), pipeline_mode=once),
            pl.BlockSpec((None, ff, d), lambda i: (layer, 0, 0), pipeline_mode=once),
            pl.BlockSpec((1, d), lambda i: (0, 0)),
        ],
        out_specs=pl.BlockSpec((tm, d), lambda i: (i, 0)),
        out_shape=jax.ShapeDtypeStruct((m, d), F32),
        compiler_params=_params(("arbitrary",)),
        name="ffn_prompt",
    )(x2, od, osb, mod_p, g2, wo, wg, wu, wd, fg)


def _proj_sample_kernel(x_ref, mod_ref, g_ref, wq_ref, wn_ref, wkv_ref, ca_ref, sb_ref, sc_ref, ct_ref, st_ref,
                        qd_ref, qs_ref, kn_ref, vn_ref, kd_ref, vd_ref, ks_ref, vs_ref,
                        *, d_model, q_scale, qs_scale):
    d = d_model
    h = _norm_mod(x_ref[...], g_ref[...], mod_ref[:, d:2 * d], mod_ref[:, 0:d]).astype(BF16)
    ca, sb, sc = ca_ref[...], sb_ref[...], sc_ref[...]
    rq = _dot(h, wq_ref[:, 0:Q_DIFF])
    for j in range(Q_DIFF // LANES):
        qd_ref[:, j * LANES:(j + 1) * LANES] = _rope_lanes(rq[:, j * LANES:(j + 1) * LANES], ca, sb, sc) * q_scale
    qs_ref[...] = _dot(h, wq_ref[:, Q_DIFF:Q_DIFF + QKV_SB]) * qs_scale
    rk = _dot(h, wn_ref[:, 0:K_DIFF])
    for j in range(K_DIFF // LANES):
        kn_ref[:, j * LANES:(j + 1) * LANES] = _rope_lanes(rk[:, j * LANES:(j + 1) * LANES], ca, sb, sc)
    vn_ref[...] = _dot(h, wn_ref[:, K_DIFF:K_DIFF + V_DIFF])
    kd, vd, ks, vs = _split_kv_rows(h, wkv_ref, ct_ref[...], st_ref[...])
    kd_ref[...] = kd
    vd_ref[...] = vd
    ks_ref[...] = ks
    vs_ref[...] = vs


def _proj_sample(layer, xs, mod_all, g1, wq_s, wn, wkv, rope_l, rope_t):
    ns, d = xs.shape
    ca, sb, sc = rope_l
    ct, st = rope_t
    kern = functools.partial(_proj_sample_kernel, d_model=d,
                             q_scale=DK_DIFF ** -0.5 * LOG2E, qs_scale=D_SB ** -0.5 * LOG2E)
    full = lambda shape: pl.BlockSpec(shape, lambda i: (0,) * len(shape))
    out_shape = [
        jax.ShapeDtypeStruct((ns, Q_DIFF), F32),
        jax.ShapeDtypeStruct((ns, QKV_SB), F32),
        jax.ShapeDtypeStruct((ns, K_DIFF), F32),
        jax.ShapeDtypeStruct((ns, V_DIFF), F32),
        jax.ShapeDtypeStruct((K_DIFF, ns), F32),
        jax.ShapeDtypeStruct((V_DIFF, ns), F32),
        jax.ShapeDtypeStruct((QKV_SB, ns), F32),
        jax.ShapeDtypeStruct((QKV_SB, ns), F32),
    ]
    return pl.pallas_call(
        kern,
        grid=(1,),
        in_specs=[
            full((ns, d)),
            pl.BlockSpec((None, ns, 6 * d), lambda i: (layer, 0, 0)),
            pl.BlockSpec((None, 1, d), lambda i: (layer, 0, 0)),
            pl.BlockSpec((None, d, Q_DIFF + QKV_SB), lambda i: (layer, 0, 0)),
            pl.BlockSpec((None, d, K_DIFF + V_DIFF), lambda i: (layer, 0, 0)),
            pl.BlockSpec((None, KV_ROWS, d), lambda i: (layer, 0, 0)),
            full((1, LANES)), full((1, LANES)), full((1, LANES)),
            full((ROT_DIM, 1)), full((ROT_DIM, 1)),
        ],
        out_specs=[full(s.shape) for s in out_shape],
        out_shape=out_shape,
        compiler_params=_params(("arbitrary",)),
        name="proj_sample",
    )(xs, mod_all, g1, wq_s, wn, wkv, ca, sb, sc, ct, st)


def _suffix_sum_lanes(x):
    rows, n = x.shape
    lane = lax.broadcasted_iota(jnp.int32, (rows, LANES), 1)
    tiles = []
    for t in range(n // LANES):
        y = x[:, t * LANES:(t + 1) * LANES]
        sh = 1
        while sh < LANES:
            y = y + jnp.where(lane < LANES - sh, pltpu.roll(y, LANES - sh, 1), 0.0)
            sh *= 2
        tiles.append(y)
    carry = jnp.zeros((rows, 1), F32)
    out = [None] * len(tiles)
    for t in range(len(tiles) - 1, -1, -1):
        out[t] = tiles[t] + carry
        carry = carry + tiles[t][:, 0:1]
    return jnp.concatenate(out, axis=1)


def _decode_attn_kernel(pt_ref, qd_ref, qs_ref, kn_ref, vn_ref, lam_ref, ckd, cvd, cks, cvs,
                        od_ref, os_ref, kd_buf, vd_buf, kt_buf, vt_buf, kr_buf, vr_buf, sem, sem_r,
                        *, layer, n_pages, page):
    i = pl.program_id(0)
    n = pl.num_programs(0)
    nslot = kd_buf.shape[0]
    slot = lax.rem(i, nslot)

    def copies(seq, sl):
        out = []
        for j in range(n_pages):
            pg = pt_ref[seq, j]
            dst = pl.ds(j * page, page)
            out.append(pltpu.make_async_copy(ckd.at[layer, pg], kd_buf.at[sl, :, dst], sem.at[sl, 0]))
            out.append(pltpu.make_async_copy(cvd.at[layer, pg], vd_buf.at[sl, :, dst], sem.at[sl, 1]))
        pg = pt_ref[seq, n_pages - 1]
        out.append(pltpu.make_async_copy(cks.at[layer, pg], kt_buf.at[sl], sem.at[sl, 2]))
        out.append(pltpu.make_async_copy(cvs.at[layer, pg], vt_buf.at[sl], sem.at[sl, 3]))
        return out

    def rest_copies(seq):
        out = []
        for j in range(n_pages - 1):
            pg = pt_ref[seq, j]
            dst = pl.ds(j * page, page)
            out.append(pltpu.make_async_copy(cks.at[layer, pg], kr_buf.at[:, dst], sem_r.at[0]))
            out.append(pltpu.make_async_copy(cvs.at[layer, pg], vr_buf.at[:, dst], sem_r.at[1]))
        return out

    @pl.when(i == 0)
    def _():
        for ahead in range(nslot - 1):
            @pl.when(ahead < n)
            def _():
                for cp in copies(ahead, ahead):
                    cp.start()

    @pl.when(i + nslot - 1 < n)
    def _():
        for cp in copies(i + nslot - 1, lax.rem(i + nslot - 1, nslot)):
            cp.start()

    for cp in copies(i, slot):
        cp.wait()

    rs = lax.broadcasted_iota(jnp.int32, (H_SB, QKV_SB), 0)
    cs = lax.broadcasted_iota(jnp.int32, (H_SB, QKV_SB), 1)
    hsel = (cs >> 6) == rs
    qsm = jnp.where(hsel, jnp.broadcast_to(qs_ref[pl.ds(i, 1), :], (H_SB, QKV_SB)), 0.0)
    z = _dot(qsm, kt_buf[slot])

    r8 = lax.broadcasted_iota(jnp.int32, (SUBLANES, K_DIFF), 0)
    c8 = lax.broadcasted_iota(jnp.int32, (SUBLANES, K_DIFF), 1)
    sel = (c8 >> 5) == r8
    qrow = qd_ref[pl.ds(i, 1), :]
    qm = jnp.concatenate(
        [jnp.where(sel, jnp.broadcast_to(qrow[:, g * K_DIFF:(g + 1) * K_DIFF], (SUBLANES, K_DIFF)), 0.0)
         for g in range(2)], axis=0)
    s = _dot(qm, kd_buf[slot])
    kn = kn_ref[pl.ds(i, 1), :]
    s_self = jnp.sum(qm * kn, axis=1, keepdims=True)
    mx = jnp.maximum(jnp.max(s, axis=1, keepdims=True), s_self)
    p = jnp.exp2(s - mx)
    p_self = jnp.exp2(s_self - mx)
    den = jnp.sum(p, axis=1, keepdims=True) + p_self
    o = _dot_nt(p, vd_buf[slot]) + p_self * vn_ref[pl.ds(i, 1), :]
    r16 = lax.broadcasted_iota(jnp.int32, (2 * SUBLANES, V_DIFF), 0)
    c16 = lax.broadcasted_iota(jnp.int32, (2 * SUBLANES, V_DIFF), 1)
    lam = lam_ref[:, 0:1]
    coef = jnp.where((r16[:, 0:1] & 1) == 1, -lam, 1.0) / den
    w = jnp.where((c16 >> 6) == ((r16 & 7) >> 1), o * coef, 0.0)
    od_ref[pl.ds(i, 1), :] = jnp.concatenate(
        [jnp.sum(w[g * SUBLANES:(g + 1) * SUBLANES], axis=0, keepdims=True) for g in range(2)], axis=1)

    suf = _suffix_sum_lanes(_log2_one_minus_beta(z))
    o_new = _dot_nt(jnp.exp2(z + suf), vt_buf[slot])
    stick = suf[:, 0:1]
    os_ref[pl.ds(i, 1), :] = jnp.sum(jnp.where(hsel, o_new, 0.0), axis=0, keepdims=True)

    @pl.when(jnp.max(stick) > -SB_STOP_LOG2)
    def _():
        for cp in rest_copies(i):
            cp.start()
        for cp in rest_copies(i):
            cp.wait()
        z_r = _dot(qsm, kr_buf[...])
        suf_r = _suffix_sum_lanes(_log2_one_minus_beta(z_r))
        o_all = o_new + _dot_nt(jnp.exp2(z_r + suf_r + stick), vr_buf[...])
        os_ref[pl.ds(i, 1), :] = jnp.sum(jnp.where(hsel, o_all, 0.0), axis=0, keepdims=True)


def _decode_attn(layer, page_table, qd, qs, kn, vn, lam, ckd, cvd, cks, cvs):
    ns, n_pages = page_table.shape
    page = ckd.shape[-1]
    past = n_pages * page
    kern = functools.partial(_decode_attn_kernel, layer=layer, n_pages=n_pages, page=page)
    res = lambda shape: pl.BlockSpec(shape, lambda i, pt: (0,) * len(shape))
    any_spec = pl.BlockSpec(memory_space=pl.ANY)
    grid_spec = pltpu.PrefetchScalarGridSpec(
        num_scalar_prefetch=1,
        grid=(ns,),
        in_specs=[
            res((ns, Q_DIFF)), res((ns, QKV_SB)), res((ns, K_DIFF)), res((ns, V_DIFF)),
            pl.BlockSpec((None, 1, LANES), lambda i, pt: (layer, 0, 0)),
            any_spec, any_spec, any_spec, any_spec,
        ],
        out_specs=[res((ns, H_DIFF * DV_DIFF)), res((ns, QKV_SB))],
        scratch_shapes=[
            pltpu.VMEM((DECODE_SLOTS, K_DIFF, past), F32),
            pltpu.VMEM((DECODE_SLOTS, V_DIFF, past), F32),
            pltpu.VMEM((DECODE_SLOTS, QKV_SB, page), F32),
            pltpu.VMEM((DECODE_SLOTS, QKV_SB, page), F32),
            pltpu.VMEM((QKV_SB, past - page), F32),
            pltpu.VMEM((QKV_SB, past - page), F32),
            pltpu.SemaphoreType.DMA((DECODE_SLOTS, 4)),
            pltpu.SemaphoreType.DMA((2,)),
        ],
    )
    return pl.pallas_call(
        kern,
        grid_spec=grid_spec,
        out_shape=[jax.ShapeDtypeStruct((ns, H_DIFF * DV_DIFF), F32), jax.ShapeDtypeStruct((ns, QKV_SB), F32)],
        compiler_params=_params(("arbitrary",)),
        name="decode_attn",
    )(page_table, qd, qs, kn, vn, lam, ckd, cvd, cks, cvs)


def _ffn_sample_kernel(x_ref, od_ref, os_ref, mod_ref, g2_ref, gs_ref, seg_ref, wo_ref, wg_ref, wu_ref, wd_ref,
                       fg_ref, o_ref, x1_sc, h_sc, acc_sc, *, d_model, out_scale, final):
    d = d_model
    c = pl.program_id(0)

    @pl.when(c == 0)
    def _():
        o = od_ref[...]
        ss = o * o
        s_hi = ss.astype(BF16)
        s_lo = (ss - s_hi.astype(F32)).astype(BF16)
        seg = _dot(s_hi, seg_ref[...]) + _dot(s_lo, seg_ref[...])
        y = (o * lax.rsqrt(seg * (1.0 / DV_DIFF) + EPS)) * gs_ref[...] * out_scale
        att = _dot(y.astype(BF16), wo_ref[0:Q_DIFF, :]) + _dot(os_ref[...].astype(BF16), wo_ref[Q_DIFF:, :])
        x1 = x_ref[...] + mod_ref[:, 2 * d:3 * d] * att
        x1_sc[...] = x1
        h_sc[...] = _norm_mod(x1, g2_ref[...], mod_ref[:, 4 * d:5 * d], mod_ref[:, 3 * d:4 * d]).astype(BF16)
        acc_sc[...] = jnp.zeros(acc_sc.shape, F32)

    h = h_sc[...]
    a = (_silu(_dot(h, wg_ref[...])) * _dot(h, wu_ref[...])).astype(BF16)
    acc_sc[...] += _dot(a, wd_ref[...])

    @pl.when(c == pl.num_programs(0) - 1)
    def _():
        y = x1_sc[...] + mod_ref[:, 5 * d:6 * d] * acc_sc[...]
        if final:
            y = _rmsnorm(y, fg_ref[...])
        o_ref[...] = y


def _ffn_sample(layer, xs, od, osb, mod_all, g2, subg8, seg, wo_s, wg, wu, wd, fg, out_scale, final):
    ns, d = xs.shape
    ff = wg.shape[-1]
    kern = functools.partial(_ffn_sample_kernel, d_model=d, out_scale=out_scale, final=final)
    full = lambda shape: pl.BlockSpec(shape, lambda c: (0,) * len(shape))
    return pl.pallas_call(
        kern,
        grid=(ff // FF_CHUNK,),
        in_specs=[
            full((ns, d)), full((ns, Q_DIFF)), full((ns, QKV_SB)),
            pl.BlockSpec((None, ns, 6 * d), lambda c: (layer, 0, 0)),
            pl.BlockSpec((None, 1, d), lambda c: (layer, 0, 0)),
            pl.BlockSpec((None, 1, Q_DIFF), lambda c: (layer, 0, 0)),
            full((Q_DIFF, Q_DIFF)),
            pl.BlockSpec((None, d, d), lambda c: (layer, 0, 0)),
            pl.BlockSpec((None, d, FF_CHUNK), lambda c: (layer, 0, c)),
            pl.BlockSpec((None, d, FF_CHUNK), lambda c: (layer, 0, c)),
            pl.BlockSpec((None, FF_CHUNK, d), lambda c: (layer, c, 0)),
            full((1, d)),
        ],
        out_specs=full((ns, d)),
        out_shape=jax.ShapeDtypeStruct((ns, d), F32),
        scratch_shapes=[pltpu.VMEM((ns, d), F32), pltpu.VMEM((ns, d), BF16), pltpu.VMEM((ns, d), F32)],
        compiler_params=_params(("arbitrary",)),
        name="ffn_sample",
    )(xs, od, osb, mod_all, g2, subg8, seg, wo_s, wg, wu, wd, fg)


def _rope_tables(pos):
    half = ROT_DIM // 2
    inv = ROPE_THETA ** (-jnp.arange(0, ROT_DIM, 2, dtype=F32) / ROT_DIM)
    ang = pos.astype(F32)[:, None] * inv[None, :]
    cos, sin = jnp.cos(ang), jnp.sin(ang)
    dl = np.arange(LANES) % DK_DIFF
    jl = dl % half
    ca = jnp.where(dl < ROT_DIM, cos[:, jl], 1.0)
    sb = jnp.where(dl < half, -sin[:, jl], 0.0)
    sc = jnp.where((dl >= half) & (dl < ROT_DIM), sin[:, jl], 0.0)
    dr = np.arange(ROT_DIM)
    ct = cos[:, dr % half].T
    st = jnp.where((dr < half)[:, None], -sin[:, dr % half].T, sin[:, dr % half].T)
    return (ca, sb, sc), (ct, st)


def kernel(x_prompt, x_sample, c_prompt, c_sample, cache_diff_k, cache_diff_v, cache_sb_k, cache_sb_v,
           page_table, norm1_g, norm2_g, w_ada, b_ada, w_in, w_o, lam_q1, lam_k1, lam_q2, lam_k2,
           subln_g, w_gate, w_up, w_down, final_g):
    batch, seq, d = x_prompt.shape
    ns = x_sample.shape[0]
    depth = w_in.shape[0]
    n_pool, page = cache_diff_k.shape[1], cache_diff_k.shape[2]
    n_pages = page_table.shape[1]
    past = n_pages * page
    assert x_sample.shape[1] == 1 and seq % min(ATT_BLOCK, seq) == 0

    c0, c1, c2, c3, c4 = Q_DIFF, Q_DIFF + K_DIFF, Q_DIFF + K_DIFF + V_DIFF, \
        Q_DIFF + K_DIFF + V_DIFF + QKV_SB, Q_DIFF + K_DIFF + V_DIFF + 2 * QKV_SB
    w_in_b = w_in.astype(BF16)
    perm = np.array([128 * h + 64 * g + e for g in range(2) for h in range(KV_DIFF) for e in range(64)])
    wq_s = jnp.concatenate([w_in_b[:, :, :c0][:, :, perm], w_in_b[:, :, c2:c3]], axis=2)
    wn = w_in_b[:, :, c0:c2]
    wkn = jnp.concatenate([w_in_b[:, :, c0:c1], w_in_b[:, :, c3:c4]], axis=2)
    w_in_t = jnp.swapaxes(w_in_b, 1, 2)
    wqt = jnp.concatenate([w_in_t[:, :c0], w_in_t[:, c2:c3]], axis=1)
    wkv = jnp.concatenate([w_in_t[:, c0:c2], w_in_t[:, c3:]], axis=1)
    wo = w_o.astype(BF16)
    wo_s = jnp.concatenate([wo[:, :Q_DIFF][:, perm], wo[:, Q_DIFF:]], axis=1)
    wg, wu, wd = w_gate.astype(BF16), w_up.astype(BF16), w_down.astype(BF16)

    g1 = norm1_g.reshape(depth, 1, d)
    g2 = norm2_g.reshape(depth, 1, d)
    fg = final_g.reshape(1, d)
    subg = subln_g.reshape(depth, DV_DIFF, 1)
    subg8 = jnp.tile(subln_g, (1, H_DIFF)).reshape(depth, 1, H_DIFF * DV_DIFF)
    lam_inits = [0.8 - 0.6 * math.exp(-0.3 * l) for l in range(depth)]
    lam = _lambdas(lam_q1, lam_k1, lam_q2, lam_k2, jnp.asarray(lam_inits, F32).reshape(depth, 1))
    lam = lam.reshape(depth, 1, LANES)

    mod_all = _modulation(jnp.concatenate([c_sample, c_prompt], axis=0), w_ada, b_ada)
    mod_p = mod_all[:, ns:].reshape(depth, batch, 1, 6 * d)

    rope_pl, rope_pt = _rope_tables(jnp.arange(seq, dtype=jnp.int32))
    rope_sl, rope_st = _rope_tables(past + jnp.arange(1, dtype=jnp.int32))

    tk = min(ATT_BLOCK, seq)
    tri_t = (np.arange(tk)[None, :] >= np.arange(tk)[:, None]).astype(np.float32)
    tri_t = jnp.asarray(np.concatenate([tri_t, tri_t], axis=1), BF16)
    segm = (np.arange(Q_DIFF)[:, None] // DV_DIFF == np.arange(Q_DIFF)[None, :] // DV_DIFF).astype(np.float32)
    segm = jnp.asarray(segm, BF16)

    def cache_rows(c):
        return jnp.transpose(c, (0, 1, 3, 4, 2)).reshape(depth, n_pool, c.shape[3] * c.shape[4], page)

    ckd, cvd, cks, cvs = (cache_rows(c) for c in (cache_diff_k, cache_diff_v, cache_sb_k, cache_sb_v))

    xp = x_prompt.reshape(batch * seq, d)
    xs = x_sample.reshape(ns, d)
    rows_s = []
    stacked = tuple(jnp.zeros((depth, batch, r, seq), F32) for r in (K_DIFF, V_DIFF, QKV_SB, QKV_SB))
    for l in range(depth):
        out_scale = 1.0 - lam_inits[l]
        final = l == depth - 1
        (qt, qst, kn, ksn, *stacked, vdb, vsb) = _proj_prompt(
            l, xp, mod_p, g1, wqt, wkv, wkn, rope_pl, rope_pt, tuple(stacked), batch, seq)
        od = _diff_attn(l, qt, kn, vdb, lam, subg, batch, seq, out_scale)
        osb = _sb_attn(qst, ksn, vsb, tri_t, batch, seq)
        xp = _ffn_prompt(l, xp, od, osb, mod_p, g2, wo, wg, wu, wd, fg, seq, final)

        (qd_s, qs_s, kn_s, vn_s, kd_s, vd_s, ks_s, vs_s) = _proj_sample(
            l, xs, mod_all, g1, wq_s, wn, wkv, rope_sl, rope_st)
        od_s, os_s = _decode_attn(l, page_table, qd_s, qs_s, kn_s, vn_s, lam, ckd, cvd, cks, cvs)
        xs = _ffn_sample(l, xs, od_s, os_s, mod_all, g2, subg8, segm, wo_s, wg, wu, wd, fg, out_scale, final)
        rows_s.append((kd_s, vd_s, ks_s, vs_s))

    def prompt_rows(idx, heads):
        a = stacked[idx]
        return jnp.transpose(a.reshape(depth, batch, heads, HEAD_DIM, seq), (0, 1, 4, 2, 3))

    def sample_rows(idx, heads):
        a = jnp.stack([r[idx] for r in rows_s])
        return jnp.transpose(a.reshape(depth, heads, HEAD_DIM, ns), (0, 3, 1, 2))[:, :, None]

    return (xp.reshape(batch, seq, d), xs.reshape(ns, 1, d),
            prompt_rows(0, KV_DIFF), prompt_rows(1, KV_DIFF), prompt_rows(2, H_SB), prompt_rows(3, H_SB),
            sample_rows(0, KV_DIFF), sample_rows(1, KV_DIFF), sample_rows(2, H_SB), sample_rows(3, H_SB))
```
